```python
import math
import jax, jax.numpy as jnp
from jax import lax
import numpy as np

D_MODEL = 4096
BATCH = 4
SEQ = 2048
DEPTH = 2

N_MIXERS = 2
EXPAND = 2
D_INNER = EXPAND * D_MODEL
D_MEM_BRANCH = D_INNER // 4
D_MIX = D_INNER - D_MEM_BRANCH
MEM_LEN = 256
MEM_HEADS = 4
MEM_HEAD_DIM = D_MEM_BRANCH // MEM_HEADS
POOL_WINDOWS = (2, 4, 8, 16)
POOL_GROUP = D_MIX // len(POOL_WINDOWS)
DIFF_QK_DIM = 128
DIFF_V_DIM = 2 * DIFF_QK_DIM
DIFF_HEADS = D_MIX // DIFF_V_DIM
Q_BLOCK = 128
RMS_EPS = 1e-6
SUBLN_EPS = 1e-5
IN_POOL = D_MIX + D_MEM_BRANCH + D_INNER
IN_DIFF = 3 * D_MIX + D_MEM_BRANCH + D_INNER

kernel_name = "hybrid_pool_diffattn_gated_memory_trunk"


def rmsnorm(x, g, eps=RMS_EPS):
    xf = x.astype(jnp.float32)
    y = xf * lax.rsqrt(jnp.mean(xf * xf, axis=-1, keepdims=True) + eps)
    return (y * g.astype(jnp.float32)).astype(x.dtype)


def alibi_slopes(n):
    def pow2(m):
        start = 2.0 ** (-8.0 / m)
        return [start ** (i + 1) for i in range(m)]
    if math.log2(n).is_integer():
        s = pow2(n)
    else:
        c = 2 ** math.floor(math.log2(n))
        s = pow2(c) + pow2(2 * c)[0::2][: n - c]
    return np.asarray(s, dtype=np.float32)


def lambda_init_fn(layer_idx):
    return 0.8 - 0.6 * math.exp(-0.3 * layer_idx)


def pool_mixer(u, pool_w, pool_scale):
    B, S, _ = u.shape
    uf = u.astype(jnp.float32)
    cs = jnp.cumsum(uf, axis=1)
    t = jnp.arange(S)
    outs = []
    for g, w in enumerate(POOL_WINDOWS):
        sl = slice(g * POOL_GROUP, (g + 1) * POOL_GROUP)
        c = cs[..., sl]
        prev = jnp.pad(c, ((0, 0), (w, 0), (0, 0)))[:, :S]
        cnt = jnp.minimum(t + 1, w).astype(jnp.float32)[None, :, None]
        outs.append((c - prev) / cnt - uf[..., sl])
    pooled = jnp.stack(outs, axis=2).astype(u.dtype)
    mixed = jnp.einsum('bsgc,gcd->bsgd', pooled, pool_w)
    return mixed.reshape(B, S, D_MIX) * pool_scale


def diff_attention(q, k, v, lam, lambda_init, subln_g):
    B, S, H, _, dk = q.shape
    nb = S // Q_BLOCK
    slopes = jnp.asarray(alibi_slopes(H))
    qb = q.reshape(B, nb, Q_BLOCK, H, 2, dk).transpose(1, 0, 2, 3, 4, 5)
    kpos = jnp.arange(S)
    scale = DIFF_QK_DIM ** -0.5

    def block(args):
        qi, i = args
        qpos = i * Q_BLOCK + jnp.arange(Q_BLOCK)
        s = jnp.einsum('bqhjd,bkhjd->bhjqk', qi, k).astype(jnp.float32) * scale
        dist = (qpos[:, None] - kpos[None, :]).astype(jnp.float32)
        s = s - slopes[None, :, None, None, None] * dist
        s = jnp.where(dist >= 0, s, -jnp.inf)
        p = jax.nn.softmax(s, axis=-1)
        a = p[:, :, 0] - lam * p[:, :, 1]
        return jnp.einsum('bhqk,bkhd->bqhd', a.astype(v.dtype), v)

    o = lax.map(block, (qb, jnp.arange(nb)))
    o = o.transpose(1, 0, 2, 3, 4).reshape(B, S, H, DIFF_V_DIM)
    o = rmsnorm(o, subln_g, SUBLN_EPS) * (1.0 - lambda_init)
    return o.reshape(B, S, H * DIFF_V_DIM)


def mem_attention(qm, mem, mem_norm_g, w_mem_kv):
    B, S, _ = qm.shape
    kv = jnp.einsum('bmd,de->bme', rmsnorm(mem, mem_norm_g), w_mem_kv)
    km = kv[..., :D_MEM_BRANCH].reshape(B, -1, MEM_HEADS, MEM_HEAD_DIM)
    vm = kv[..., D_MEM_BRANCH:].reshape(B, -1, MEM_HEADS, MEM_HEAD_DIM)
    qh = qm.reshape(B, S, MEM_HEADS, MEM_HEAD_DIM)
    s = jnp.einsum('bqhd,bkhd->bhqk', qh, km).astype(jnp.float32) * MEM_HEAD_DIM ** -0.5
    p = jax.nn.softmax(s, axis=-1)
    o = jnp.einsum('bhqk,bkhd->bqhd', p.astype(vm.dtype), vm)
    return o.reshape(B, S, D_MEM_BRANCH)


def pool_layer(x, mem, norm_g, w_in, pool_w, pool_scale, mem_norm_g, w_mem_kv, w_out):
    h = rmsnorm(x, norm_g)
    proj = jnp.einsum('bsd,de->bse', h, w_in)
    u = proj[..., :D_MIX]
    qm = proj[..., D_MIX:D_MIX + D_MEM_BRANCH]
    z = proj[..., D_MIX + D_MEM_BRANCH:]
    y = jnp.concatenate([pool_mixer(u, pool_w, pool_scale),
                         mem_attention(qm, mem, mem_norm_g, w_mem_kv)], axis=-1)
    return x + jnp.einsum('bse,ed->bsd', y * jax.nn.silu(z), w_out)


def diff_layer(x, mem, layer_idx, norm_g, w_in, lq1, lk1, lq2, lk2, subln_g,
               mem_norm_g, w_mem_kv, w_out):
    B, S, _ = x.shape
    h = rmsnorm(x, norm_g)
    proj = jnp.einsum('bsd,de->bse', h, w_in)
    q = proj[..., :D_MIX].reshape(B, S, DIFF_HEADS, 2, DIFF_QK_DIM)
    k = proj[..., D_MIX:2 * D_MIX].reshape(B, S, DIFF_HEADS, 2, DIFF_QK_DIM)
    v = proj[..., 2 * D_MIX:3 * D_MIX].reshape(B, S, DIFF_HEADS, DIFF_V_DIM)
    qm = proj[..., 3 * D_MIX:3 * D_MIX + D_MEM_BRANCH]
    z = proj[..., 3 * D_MIX + D_MEM_BRANCH:]
    lam_init = lambda_init_fn(layer_idx)
    f32 = jnp.float32
    lam = (jnp.exp(jnp.sum(lq1.astype(f32) * lk1.astype(f32)))
           - jnp.exp(jnp.sum(lq2.astype(f32) * lk2.astype(f32))) + lam_init)
    y = jnp.concatenate([diff_attention(q, k, v, lam, lam_init, subln_g),
                         mem_attention(qm, mem, mem_norm_g, w_mem_kv)], axis=-1)
    return x + jnp.einsum('bse,ed->bsd', y * jax.nn.silu(z), w_out)


def setup_inputs(seed: int = 0) -> dict:
    key = jax.random.key(seed)
    ks = jax.random.split(key, 24)
    f32 = jnp.float32
    nrm = lambda k, shape, s: jax.random.normal(k, shape, f32) * s
    gain = lambda k, n: 1.0 + 0.02 * jax.random.normal(k, (n,), f32)
    return {
        "x": jax.random.normal(ks[0], (BATCH, SEQ, D_MODEL), f32),
        "mem": jax.random.normal(ks[1], (BATCH, MEM_LEN, D_MODEL), f32),
        "l0_norm_g": gain(ks[2], D_MODEL),
        "l0_w_in": nrm(ks[3], (D_MODEL, IN_POOL), D_MODEL ** -0.5),
        "l0_pool_w": nrm(ks[4], (len(POOL_WINDOWS), POOL_GROUP, POOL_GROUP), POOL_GROUP ** -0.5),
        "l0_pool_scale": gain(ks[5], D_MIX),
        "l0_mem_norm_g": gain(ks[6], D_MODEL),
        "l0_w_mem_kv": nrm(ks[7], (D_MODEL, 2 * D_MEM_BRANCH), D_MODEL ** -0.5),
        "l0_w_out": nrm(ks[8], (D_INNER, D_MODEL), D_INNER ** -0.5),
        "l1_norm_g": gain(ks[9], D_MODEL),
        "l1_w_in": nrm(ks[10], (D_MODEL, IN_DIFF), D_MODEL ** -0.5),
        "l1_lambda_q1": nrm(ks[11], (DIFF_QK_DIM,), 0.1),
        "l1_lambda_k1": nrm(ks[12], (DIFF_QK_DIM,), 0.1),
        "l1_lambda_q2": nrm(ks[13], (DIFF_QK_DIM,), 0.1),
        "l1_lambda_k2": nrm(ks[14], (DIFF_QK_DIM,), 0.1),
        "l1_subln_g": gain(ks[15], DIFF_V_DIM),
        "l1_mem_norm_g": gain(ks[16], D_MODEL),
        "l1_w_mem_kv": nrm(ks[17], (D_MODEL, 2 * D_MEM_BRANCH), D_MODEL ** -0.5),
        "l1_w_out": nrm(ks[18], (D_INNER, D_MODEL), D_INNER ** -0.5),
        "final_norm_g": gain(ks[19], D_MODEL),
    }


def reference(x, mem, l0_norm_g, l0_w_in, l0_pool_w, l0_pool_scale, l0_mem_norm_g,
              l0_w_mem_kv, l0_w_out, l1_norm_g, l1_w_in, l1_lambda_q1, l1_lambda_k1,
              l1_lambda_q2, l1_lambda_k2, l1_subln_g, l1_mem_norm_g, l1_w_mem_kv,
              l1_w_out, final_norm_g):
    pool_params = [(l0_norm_g, l0_w_in, l0_pool_w, l0_pool_scale, l0_mem_norm_g,
                    l0_w_mem_kv, l0_w_out)]
    diff_params = [(l1_norm_g, l1_w_in, l1_lambda_q1, l1_lambda_k1, l1_lambda_q2,
                    l1_lambda_k2, l1_subln_g, l1_mem_norm_g, l1_w_mem_kv, l1_w_out)]
    for i in range(DEPTH):
        if i % N_MIXERS == 0:
            x = pool_layer(x, mem, *pool_params[i // N_MIXERS])
        else:
            x = diff_layer(x, mem, i, *diff_params[i // N_MIXERS])
    return rmsnorm(x, final_norm_g)
```

```python
import functools
import math

import numpy as np
import jax
import jax.numpy as jnp
from jax import lax
from jax.experimental import pallas as pl
from jax.experimental.pallas import tpu as pltpu

F32 = jnp.float32
BF16 = jnp.bfloat16

D_MODEL = 4096
D_INNER = 2 * D_MODEL
D_MEM = D_INNER // 4
D_MIX = D_INNER - D_MEM
MEM_HEADS = 4
MEM_HEAD_DIM = D_MEM // MEM_HEADS
POOL_WINDOWS = (2, 4, 8, 16)
POOL_GROUP = D_MIX // len(POOL_WINDOWS)
POOL_HALO = 16
DIFF_QK = 128
DIFF_V = 2 * DIFF_QK
DIFF_HEADS = D_MIX // DIFF_V
RMS_EPS = 1e-6
SUBLN_EPS = 1e-5
MASK_VALUE = -1e30

V7X_VMEM_BYTES = 64 * 1024 * 1024
VMEM_LIMIT_BYTES = V7X_VMEM_BYTES - 8 * 1024 * 1024


def _params(*semantics):
    return pltpu.CompilerParams(dimension_semantics=semantics,
                                vmem_limit_bytes=VMEM_LIMIT_BYTES)


def _silu(z):
    return z / (1.0 + jnp.exp(-z))


def _alibi_slopes(n):
    def pow2(m):
        start = 2.0 ** (-8.0 / m)
        return [start ** (i + 1) for i in range(m)]
    if math.log2(n).is_integer():
        s = pow2(n)
    else:
        c = 2 ** math.floor(math.log2(n))
        s = pow2(c) + pow2(2 * c)[0::2][: n - c]
    return np.asarray(s, dtype=np.float32)


def _rmsnorm_kernel(x_ref, g_ref, o_ref, *, eps):
    x = x_ref[...].astype(F32)
    ms = jnp.mean(x * x, axis=-1, keepdims=True)
    o_ref[...] = (x * lax.rsqrt(ms + eps) * g_ref[...]).astype(o_ref.dtype)


def _rmsnorm(x, g, out_dtype, *, eps=RMS_EPS, tm=256):
    m, d = x.shape
    return pl.pallas_call(
        functools.partial(_rmsnorm_kernel, eps=eps),
        grid=(m // tm,),
        in_specs=[pl.BlockSpec((tm, d), lambda i: (i, 0)),
                  pl.BlockSpec((1, d), lambda i: (0, 0))],
        out_specs=pl.BlockSpec((tm, d), lambda i: (i, 0)),
        out_shape=jax.ShapeDtypeStruct((m, d), out_dtype),
        compiler_params=_params("parallel"),
        name="rmsnorm",
    )(x, g.reshape(1, d).astype(F32))


def _matmul_kernel(a_ref, w_ref, o_ref):
    o_ref[...] = jnp.dot(a_ref[...], w_ref[...],
                         preferred_element_type=F32).astype(o_ref.dtype)


def _matmul(a, w, out_dtype=BF16, *, bm=1024, bn=1024):
    m, k = a.shape
    _, n = w.shape
    bm = min(bm, m)
    return pl.pallas_call(
        _matmul_kernel,
        grid=(m // bm, n // bn),
        in_specs=[pl.BlockSpec((bm, k), lambda i, j: (i, 0)),
                  pl.BlockSpec((k, bn), lambda i, j: (0, j))],
        out_specs=pl.BlockSpec((bm, bn), lambda i, j: (i, j)),
        out_shape=jax.ShapeDtypeStruct((m, n), out_dtype),
        compiler_params=_params("parallel", "arbitrary"),
        name="matmul",
    )(a, w)


OUT_BK = D_MEM
OUT_MIX_CHUNKS = D_MIX // OUT_BK


def _out_proj_kernel(ymix_ref, ymem_ref, w_ref, x_ref, o_ref):
    kk = pl.program_id(2)

    @pl.when(kk == 0)
    def _():
        o_ref[...] = x_ref[...] + jnp.dot(ymix_ref[...], w_ref[...],
                                          preferred_element_type=F32)

    @pl.when(jnp.logical_and(kk > 0, kk < OUT_MIX_CHUNKS))
    def _():
        o_ref[...] += jnp.dot(ymix_ref[...], w_ref[...],
                              preferred_element_type=F32)

    @pl.when(kk == OUT_MIX_CHUNKS)
    def _():
        o_ref[...] += jnp.dot(ymem_ref[...], w_ref[...],
                              preferred_element_type=F32)


def _out_proj(ymix, ymem, w, x, *, bm=1024, bn=1024):
    m = x.shape[0]
    n = w.shape[1]
    last = OUT_MIX_CHUNKS - 1
    return pl.pallas_call(
        _out_proj_kernel,
        grid=(m // bm, n // bn, OUT_MIX_CHUNKS + 1),
        in_specs=[
            pl.BlockSpec((bm, OUT_BK), lambda i, j, k: (i, jnp.minimum(k, last))),
            pl.BlockSpec((bm, OUT_BK), lambda i, j, k: (i, 0)),
            pl.BlockSpec((OUT_BK, bn), lambda i, j, k: (k, j)),
            pl.BlockSpec((bm, bn), lambda i, j, k: (i, j)),
        ],
        out_specs=pl.BlockSpec((bm, bn), lambda i, j, k: (i, j)),
        out_shape=jax.ShapeDtypeStruct((m, n), F32),
        compiler_params=_params("parallel", "parallel", "arbitrary"),
        name="out_proj",
    )(ymix, ymem, w, x)


def _pool_kernel(u_ref, prev_ref, z_ref, w_ref, scale_ref, o_ref, pooled_ref, *, ts):
    g = pl.program_id(0)
    si = pl.program_id(2)

    x = u_ref[...].astype(F32)
    prev = prev_ref[...].astype(F32)
    prev = jnp.where(si > 0, prev, 0.0)
    xs = jnp.concatenate([prev, x], axis=0)
    t = si * ts + lax.broadcasted_iota(jnp.int32, (ts, 1), 0)

    for idx, window in enumerate(POOL_WINDOWS):
        @pl.when(g == idx)
        def _(window=window):
            s = xs
            k = 1
            while k < window:
                s = s + pltpu.roll(s, k, 0)
                k *= 2
            cnt = jnp.minimum(t + 1, window).astype(F32)
            pooled = s[POOL_HALO:] / cnt - x
            pooled_ref[...] = pooled.astype(pooled_ref.dtype)

    mixed = jnp.dot(pooled_ref[...], w_ref[0], preferred_element_type=F32)
    z = z_ref[...].astype(F32)
    o_ref[...] = (mixed * scale_ref[...] * _silu(z)).astype(o_ref.dtype)


def _pool_mixer(u, z, pool_w, pool_scale, *, batch, seq, ts=512):
    t_total = u.shape[0]
    c = POOL_GROUP
    n_s = seq // ts
    halo_blocks_per_tile = ts // POOL_HALO

    def prev_map(g, b, si):
        first = (b * n_s + si) * halo_blocks_per_tile
        return (jnp.maximum(first - 1, 0), g)

    return pl.pallas_call(
        functools.partial(_pool_kernel, ts=ts),
        grid=(len(POOL_WINDOWS), batch, n_s),
        in_specs=[
            pl.BlockSpec((ts, c), lambda g, b, si: (b * n_s + si, g)),
            pl.BlockSpec((POOL_HALO, c), prev_map),
            pl.BlockSpec((ts, c), lambda g, b, si: (b * n_s + si, g)),
            pl.BlockSpec((1, c, c), lambda g, b, si: (g, 0, 0)),
            pl.BlockSpec((1, c), lambda g, b, si: (0, g)),
        ],
        out_specs=pl.BlockSpec((ts, c), lambda g, b, si: (b * n_s + si, g)),
        out_shape=jax.ShapeDtypeStruct((t_total, D_MIX), BF16),
        scratch_shapes=[pltpu.VMEM((ts, c), BF16)],
        compiler_params=_params("arbitrary", "parallel", "arbitrary"),
        name="pool_mixer",
    )(u, u, z, pool_w, pool_scale.reshape(1, D_MIX).astype(F32))


def _diff_attn_kernel(slope_ref, lq1_ref, lk1_ref, lq2_ref, lk2_ref,
                      q_ref, k_ref, v_ref, z_ref, g_ref, o_ref,
                      m_ref, l_ref, acc_ref, *, tq, lam_init):
    h = pl.program_id(1)
    qi = pl.program_id(2)
    slope = slope_ref[h]
    scale = DIFF_QK ** -0.5
    q0 = qi * tq

    m_ref[...] = jnp.full(m_ref.shape, MASK_VALUE, F32)
    l_ref[...] = jnp.zeros(l_ref.shape, F32)
    acc_ref[...] = jnp.zeros(acc_ref.shape, F32)

    col = lax.broadcasted_iota(jnp.int32, (1, tq), 1)
    row = lax.broadcasted_iota(jnp.int32, (tq, 1), 0)

    def step(ki, masked):
        k0 = pl.multiple_of(ki * tq, tq)
        kb = k_ref[pl.ds(k0, tq), :]
        vb = v_ref[pl.ds(k0, tq), :]
        bias = slope * (col + (k0 - q0)).astype(F32)
        for j in range(2):
            sl = slice(j * DIFF_QK, (j + 1) * DIFF_QK)
            s = lax.dot_general(q_ref[:, sl], kb[:, sl], (((1,), (1,)), ((), ())),
                                preferred_element_type=F32)
            s = s * scale + bias
            if masked:
                s = jnp.where(col <= row, s, MASK_VALUE)
            m_old = m_ref[j]
            m_new = jnp.maximum(m_old, jnp.max(s, axis=-1, keepdims=True))
            p = jnp.exp(s - m_new)
            alpha = jnp.exp(m_old - m_new)
            l_ref[j] = alpha * l_ref[j] + jnp.sum(p, axis=-1, keepdims=True)
            acc_ref[j] = alpha * acc_ref[j] + jnp.dot(p.astype(BF16), vb,
                                                      preferred_element_type=F32)
            m_ref[j] = m_new

    def body(ki, carry):
        step(ki, False)
        return carry

    lax.fori_loop(0, qi, body, 0)
    step(qi, True)

    f = lambda ref: ref[...].astype(F32)
    lam = (jnp.exp(jnp.sum(f(lq1_ref) * f(lk1_ref), keepdims=True))
           - jnp.exp(jnp.sum(f(lq2_ref) * f(lk2_ref), keepdims=True)) + lam_init)
    o = acc_ref[0] / l_ref[0] - lam * (acc_ref[1] / l_ref[1])
    ms = jnp.mean(o * o, axis=-1, keepdims=True)
    y = (o * lax.rsqrt(ms + SUBLN_EPS) * g_ref[...]) * (1.0 - lam_init)
    z = z_ref[...].astype(F32)
    o_ref[...] = (y * _silu(z)).astype(o_ref.dtype)


def _diff_attention(qkv, z, lam_vecs, subln_g, *, batch, seq, lam_init, tq=512):
    t_total = qkv.shape[0]
    n_q = seq // tq
    slopes = jnp.asarray(_alibi_slopes(DIFF_HEADS))
    vec_spec = pl.BlockSpec((1, DIFF_QK), lambda b, h, qi: (0, 0))
    lam_vecs = [v.reshape(1, DIFF_QK).astype(F32) for v in lam_vecs]
    return pl.pallas_call(
        functools.partial(_diff_attn_kernel, tq=tq, lam_init=lam_init),
        grid=(batch, DIFF_HEADS, n_q),
        in_specs=[
            pl.BlockSpec(memory_space=pltpu.SMEM),
            vec_spec, vec_spec, vec_spec, vec_spec,
            pl.BlockSpec((tq, DIFF_V), lambda b, h, qi: (b * n_q + qi, h)),
            pl.BlockSpec((seq, DIFF_V), lambda b, h, qi: (b, DIFF_HEADS + h)),
            pl.BlockSpec((seq, DIFF_V), lambda b, h, qi: (b, 2 * DIFF_HEADS + h)),
            pl.BlockSpec((tq, DIFF_V), lambda b, h, qi: (b * n_q + qi, h)),
            pl.BlockSpec((1, DIFF_V), lambda b, h, qi: (0, 0)),
        ],
        out_specs=pl.BlockSpec((tq, DIFF_V), lambda b, h, qi: (b * n_q + qi, h)),
        out_shape=jax.ShapeDtypeStruct((t_total, D_MIX), BF16),
        scratch_shapes=[pltpu.VMEM((2, tq, 1), F32),
                        pltpu.VMEM((2, tq, 1), F32),
                        pltpu.VMEM((2, tq, DIFF_V), F32)],
        compiler_params=_params("parallel", "parallel", "arbitrary"),
        name="diff_attention",
    )(slopes, *lam_vecs, qkv, qkv, qkv, z, subln_g.reshape(1, DIFF_V).astype(F32))


def _mem_attn_kernel(q_ref, k_ref, v_ref, z_ref, o_ref):
    scale = MEM_HEAD_DIM ** -0.5
    for h in range(MEM_HEADS):
        sl = slice(h * MEM_HEAD_DIM, (h + 1) * MEM_HEAD_DIM)
        s = lax.dot_general(q_ref[:, sl], k_ref[:, sl], (((1,), (1,)), ((), ())),
                            preferred_element_type=F32) * scale
        m = jnp.max(s, axis=-1, keepdims=True)
        p = jnp.exp(s - m)
        l = jnp.sum(p, axis=-1, keepdims=True)
        o = jnp.dot(p.astype(BF16), v_ref[:, sl], preferred_element_type=F32) / l
        z = z_ref[:, sl].astype(F32)
        o_ref[:, sl] = (o * _silu(z)).astype(o_ref.dtype)


def _mem_attention(qm, kv, z, *, batch, seq, mem_len, tq=512):
    t_total = qm.shape[0]
    n_q = seq // tq
    z_block = D_MIX // D_MEM
    return pl.pallas_call(
        _mem_attn_kernel,
        grid=(batch, n_q),
        in_specs=[
            pl.BlockSpec((tq, D_MEM), lambda b, qi: (b * n_q + qi, 0)),
            pl.BlockSpec((mem_len, D_MEM), lambda b, qi: (b, 0)),
            pl.BlockSpec((mem_len, D_MEM), lambda b, qi: (b, 1)),
            pl.BlockSpec((tq, D_MEM), lambda b, qi: (b * n_q + qi, z_block)),
        ],
        out_specs=pl.BlockSpec((tq, D_MEM), lambda b, qi: (b * n_q + qi, 0)),
        out_shape=jax.ShapeDtypeStruct((t_total, D_MEM), BF16),
        compiler_params=_params("parallel", "arbitrary"),
        name="mem_attention",
    )(qm, kv, kv, z)


def _lambda_init(layer_idx):
    return 0.8 - 0.6 * math.exp(-0.3 * layer_idx)


def kernel(x, mem, l0_norm_g, l0_w_in, l0_pool_w, l0_pool_scale, l0_mem_norm_g,
           l0_w_mem_kv, l0_w_out, l1_norm_g, l1_w_in, l1_lambda_q1, l1_lambda_k1,
           l1_lambda_q2, l1_lambda_k2, l1_subln_g, l1_mem_norm_g, l1_w_mem_kv,
           l1_w_out, final_norm_g):
    batch, seq, d = x.shape
    mem_len = mem.shape[1]
    x2 = x.reshape(batch * seq, d)
    mem2 = mem.reshape(batch * mem_len, d)
    dims = dict(batch=batch, seq=seq)

    h = _rmsnorm(x2, l0_norm_g, BF16)
    w_in = l0_w_in.astype(BF16)
    u = _matmul(h, w_in[:, :D_MIX])
    qm = _matmul(h, w_in[:, D_MIX:D_MIX + D_MEM])
    z = _matmul(h, w_in[:, D_MIX + D_MEM:])
    kv = _matmul(_rmsnorm(mem2, l0_mem_norm_g, BF16), l0_w_mem_kv.astype(BF16))
    y_mix = _pool_mixer(u, z, l0_pool_w.astype(BF16), l0_pool_scale, **dims)
    y_mem = _mem_attention(qm, kv, z, mem_len=mem_len, **dims)
    x2 = _out_proj(y_mix, y_mem, l0_w_out.astype(BF16), x2)

    h = _rmsnorm(x2, l1_norm_g, BF16)
    w_in = l1_w_in.astype(BF16)
    qkv = _matmul(h, w_in[:, :3 * D_MIX])
    qm = _matmul(h, w_in[:, 3 * D_MIX:3 * D_MIX + D_MEM])
    z = _matmul(h, w_in[:, 3 * D_MIX + D_MEM:])
    kv = _matmul(_rmsnorm(mem2, l1_mem_norm_g, BF16), l1_w_mem_kv.astype(BF16))
    y_mix = _diff_attention(qkv, z, (l1_lambda_q1, l1_lambda_k1, l1_lambda_q2, l1_lambda_k2),
                            l1_subln_g, lam_init=_lambda_init(1), **dims)
    y_mem = _mem_attention(qm, kv, z, mem_len=mem_len, **dims)
    x2 = _out_proj(y_mix, y_mem, l1_w_out.astype(BF16), x2)

    return _rmsnorm(x2, final_norm_g, F32).reshape(batch, seq, d)
```

```python
import functools
import math

import numpy as np
import jax
import jax.numpy as jnp
from jax import lax
from jax.experimental import pallas as pl
from jax.experimental.pallas import tpu as pltpu

F32 = jnp.float32
BF16 = jnp.bfloat16

D_MODEL = 4096
D_INNER = 2 * D_MODEL
D_MEM = D_INNER // 4
D_MIX = D_INNER - D_MEM
MEM_HEADS = 4
MEM_HEAD_DIM = D_MEM // MEM_HEADS
POOL_WINDOWS = (2, 4, 8, 16)
POOL_GROUP = D_MIX // len(POOL_WINDOWS)
POOL_HALO = 16
DIFF_QK = 128
DIFF_V = 2 * DIFF_QK
DIFF_HEADS = D_MIX // DIFF_V
RMS_EPS = 1e-6
SUBLN_EPS = 1e-5
MASK_VALUE = -1e30
LOG2_E = math.log2(math.e)
LANES = 128

V7X_VMEM_BYTES = 64 * 1024 * 1024
VMEM_LIMIT_BYTES = V7X_VMEM_BYTES - 8 * 1024 * 1024


def _params(*semantics):
    return pltpu.CompilerParams(dimension_semantics=semantics,
                                vmem_limit_bytes=VMEM_LIMIT_BYTES)


def _silu(z):
    return z / (1.0 + jnp.exp(-z))


def _alibi_slopes(n):
    def pow2(m):
        start = 2.0 ** (-8.0 / m)
        return [start ** (i + 1) for i in range(m)]
    if math.log2(n).is_integer():
        s = pow2(n)
    else:
        c = 2 ** math.floor(math.log2(n))
        s = pow2(c) + pow2(2 * c)[0::2][: n - c]
    return np.asarray(s, dtype=np.float32)


def _rmsnorm_kernel(x_ref, g_ref, o_ref, *, eps):
    x = x_ref[...].astype(F32)
    ms = jnp.mean(x * x, axis=-1, keepdims=True)
    o_ref[...] = (x * lax.rsqrt(ms + eps) * g_ref[...]).astype(o_ref.dtype)


def _rmsnorm(x, g, out_dtype, *, eps=RMS_EPS, tm=256):
    m, d = x.shape
    return pl.pallas_call(
        functools.partial(_rmsnorm_kernel, eps=eps),
        grid=(m // tm,),
        in_specs=[pl.BlockSpec((tm, d), lambda i: (i, 0)),
                  pl.BlockSpec((1, d), lambda i: (0, 0))],
        out_specs=pl.BlockSpec((tm, d), lambda i: (i, 0)),
        out_shape=jax.ShapeDtypeStruct((m, d), out_dtype),
        compiler_params=_params("parallel"),
        name="rmsnorm",
    )(x, g.reshape(1, d).astype(F32))


def _matmul_kernel(a_ref, w_ref, o_ref, wb_ref):
    @pl.when(pl.program_id(1) == 0)
    def _():
        wb_ref[...] = w_ref[...].astype(BF16)

    o_ref[...] = jnp.dot(a_ref[...], wb_ref[...],
                         preferred_element_type=F32).astype(o_ref.dtype)


def _matmul(a, w, col0, ncols, out_dtype=BF16, *, bm=1024, bn=512):
    m, k = a.shape
    bm = min(bm, m)
    assert col0 % bn == 0 and ncols % bn == 0 and m % bm == 0
    j0 = col0 // bn
    return pl.pallas_call(
        _matmul_kernel,
        grid=(ncols // bn, m // bm),
        in_specs=[pl.BlockSpec((bm, k), lambda j, i: (i, 0)),
                  pl.BlockSpec((k, bn), lambda j, i: (0, j0 + j))],
        out_specs=pl.BlockSpec((bm, bn), lambda j, i: (i, j)),
        out_shape=jax.ShapeDtypeStruct((m, ncols), out_dtype),
        scratch_shapes=[pltpu.VMEM((k, bn), BF16)],
        compiler_params=_params("parallel", "arbitrary"),
        name="matmul",
    )(a, w)


OUT_BK = D_MEM
OUT_MIX_CHUNKS = D_MIX // OUT_BK


def _out_proj_kernel(ymix_ref, ymem_ref, w_ref, x_ref, o_ref):
    kk = pl.program_id(2)

    @pl.when(kk == 0)
    def _():
        o_ref[...] = x_ref[...] + jnp.dot(ymix_ref[...], w_ref[...],
                                          preferred_element_type=F32)

    @pl.when(jnp.logical_and(kk > 0, kk < OUT_MIX_CHUNKS))
    def _():
        o_ref[...] += jnp.dot(ymix_ref[...], w_ref[...],
                              preferred_element_type=F32)

    @pl.when(kk == OUT_MIX_CHUNKS)
    def _():
        o_ref[...] += jnp.dot(ymem_ref[...], w_ref[...],
                              preferred_element_type=F32)


def _out_proj(ymix, ymem, w, x, *, bm=1024, bn=1024):
    m = x.shape[0]
    n = w.shape[1]
    last = OUT_MIX_CHUNKS - 1
    return pl.pallas_call(
        _out_proj_kernel,
        grid=(m // bm, n // bn, OUT_MIX_CHUNKS + 1),
        in_specs=[
            pl.BlockSpec((bm, OUT_BK), lambda i, j, k: (i, jnp.minimum(k, last))),
            pl.BlockSpec((bm, OUT_BK), lambda i, j, k: (i, 0)),
            pl.BlockSpec((OUT_BK, bn), lambda i, j, k: (k, j)),
            pl.BlockSpec((bm, bn), lambda i, j, k: (i, j)),
        ],
        out_specs=pl.BlockSpec((bm, bn), lambda i, j, k: (i, j)),
        out_shape=jax.ShapeDtypeStruct((m, n), F32),
        compiler_params=_params("parallel", "parallel", "arbitrary"),
        name="out_proj",
    )(ymix, ymem, w, x)


def _pool_kernel(u_ref, prev_ref, z_ref, w_ref, scale_ref, o_ref, pooled_ref, *, ts):
    g = pl.program_id(0)
    si = pl.program_id(2)

    x = u_ref[...].astype(F32)
    prev = prev_ref[...].astype(F32)
    prev = jnp.where(si > 0, prev, 0.0)
    xs = jnp.concatenate([prev, x], axis=0)
    t = si * ts + lax.broadcasted_iota(jnp.int32, (ts, 1), 0)

    for idx, window in enumerate(POOL_WINDOWS):
        @pl.when(g == idx)
        def _(window=window):
            s = xs
            k = 1
            while k < window:
                s = s + pltpu.roll(s, k, 0)
                k *= 2
            cnt = jnp.minimum(t + 1, window).astype(F32)
            pooled = s[POOL_HALO:] / cnt - x
            pooled_ref[...] = pooled.astype(pooled_ref.dtype)

    mixed = jnp.dot(pooled_ref[...], w_ref[0], preferred_element_type=F32)
    z = z_ref[...].astype(F32)
    o_ref[...] = (mixed * scale_ref[...] * _silu(z)).astype(o_ref.dtype)


def _pool_mixer(u, z, pool_w, pool_scale, *, batch, seq, ts=512):
    t_total = u.shape[0]
    c = POOL_GROUP
    n_s = seq // ts
    halo_blocks_per_tile = ts // POOL_HALO

    def prev_map(g, b, si):
        first = (b * n_s + si) * halo_blocks_per_tile
        return (jnp.maximum(first - 1, 0), g)

    return pl.pallas_call(
        functools.partial(_pool_kernel, ts=ts),
        grid=(len(POOL_WINDOWS), batch, n_s),
        in_specs=[
            pl.BlockSpec((ts, c), lambda g, b, si: (b * n_s + si, g)),
            pl.BlockSpec((POOL_HALO, c), prev_map),
            pl.BlockSpec((ts, c), lambda g, b, si: (b * n_s + si, g)),
            pl.BlockSpec((1, c, c), lambda g, b, si: (g, 0, 0)),
            pl.BlockSpec((1, c), lambda g, b, si: (0, g)),
        ],
        out_specs=pl.BlockSpec((ts, c), lambda g, b, si: (b * n_s + si, g)),
        out_shape=jax.ShapeDtypeStruct((t_total, D_MIX), BF16),
        scratch_shapes=[pltpu.VMEM((ts, c), BF16)],
        compiler_params=_params("arbitrary", "parallel", "arbitrary"),
        name="pool_mixer",
    )(u, u, z, pool_w, pool_scale.reshape(1, D_MIX).astype(F32))


def _diff_attn_kernel(slope_ref, lq1_ref, lk1_ref, lq2_ref, lk2_ref,
                      q_ref, k_ref, v_ref, z_ref, g_ref, o_ref,
                      m_ref, l_ref, acc_ref, *, tq, lam_init):
    h = pl.program_id(1)
    qi = pl.program_id(2)
    slope2 = slope_ref[h] * LOG2_E
    scale2 = DIFF_QK ** -0.5 * LOG2_E
    q0 = qi * tq
    lane_reps = tq // LANES

    m_ref[...] = jnp.full(m_ref.shape, MASK_VALUE, F32)
    l_ref[...] = jnp.zeros(l_ref.shape, F32)
    acc_ref[...] = jnp.zeros(acc_ref.shape, F32)

    col = lax.broadcasted_iota(jnp.int32, (1, tq), 1)
    row = lax.broadcasted_iota(jnp.int32, (tq, 1), 0)

    def step(ki, masked):
        k0 = pl.multiple_of(ki * tq, tq)
        kb = k_ref[pl.ds(k0, tq), :]
        vb = v_ref[pl.ds(k0, tq), :]
        bias = slope2 * (col + (k0 - q0)).astype(F32)
        for j in range(2):
            sl = slice(j * DIFF_QK, (j + 1) * DIFF_QK)
            s = lax.dot_general(q_ref[:, sl], kb[:, sl], (((1,), (1,)), ((), ())),
                                preferred_element_type=F32)
            s = s * scale2 + bias
            if masked:
                s = jnp.where(col <= row, s, MASK_VALUE)
            m_old = m_ref[j]
            m_new = jnp.maximum(m_old, jnp.max(s, axis=-1, keepdims=True))
            p = jnp.exp2(s - pltpu.repeat(m_new, lane_reps, 1))
            alpha = jnp.exp2(m_old - m_new)
            l_ref[j] = alpha * l_ref[j] + jnp.sum(p, axis=-1, keepdims=True)
            acc_ref[j] = (pltpu.repeat(alpha, DIFF_V // LANES, 1) * acc_ref[j]
                          + jnp.dot(p.astype(BF16), vb, preferred_element_type=F32))
            m_ref[j] = m_new

    def body(ki, carry):
        step(ki, False)
        return carry

    lax.fori_loop(0, qi, body, 0)
    step(qi, True)

    f = lambda ref: ref[...].astype(F32)
    lam = (jnp.exp(jnp.sum(f(lq1_ref) * f(lk1_ref), keepdims=True))
           - jnp.exp(jnp.sum(f(lq2_ref) * f(lk2_ref), keepdims=True)) + lam_init)
    norm = lambda j: acc_ref[j] / pltpu.repeat(l_ref[j], DIFF_V // LANES, 1)
    o = norm(0) - lam * norm(1)
    ms = jnp.mean(o * o, axis=-1, keepdims=True)
    y = (o * lax.rsqrt(ms + SUBLN_EPS) * g_ref[...]) * (1.0 - lam_init)
    z = z_ref[...].astype(F32)
    o_ref[...] = (y * _silu(z)).astype(o_ref.dtype)


def _diff_attention(qkv, z, lam_vecs, subln_g, *, batch, seq, lam_init, tq=512):
    t_total = qkv.shape[0]
    n_q = seq // tq
    slopes = jnp.asarray(_alibi_slopes(DIFF_HEADS))
    vec_spec = pl.BlockSpec((1, DIFF_QK), lambda b, h, qi: (0, 0))
    lam_vecs = [v.reshape(1, DIFF_QK).astype(F32) for v in lam_vecs]
    return pl.pallas_call(
        functools.partial(_diff_attn_kernel, tq=tq, lam_init=lam_init),
        grid=(batch, DIFF_HEADS, n_q),
        in_specs=[
            pl.BlockSpec(memory_space=pltpu.SMEM),
            vec_spec, vec_spec, vec_spec, vec_spec,
            pl.BlockSpec((tq, DIFF_V), lambda b, h, qi: (b * n_q + qi, h)),
            pl.BlockSpec((seq, DIFF_V), lambda b, h, qi: (b, DIFF_HEADS + h)),
            pl.BlockSpec((seq, DIFF_V), lambda b, h, qi: (b, 2 * DIFF_HEADS + h)),
            pl.BlockSpec((tq, DIFF_V), lambda b, h, qi: (b * n_q + qi, h)),
            pl.BlockSpec((1, DIFF_V), lambda b, h, qi: (0, 0)),
        ],
        out_specs=pl.BlockSpec((tq, DIFF_V), lambda b, h, qi: (b * n_q + qi, h)),
        out_shape=jax.ShapeDtypeStruct((t_total, D_MIX), BF16),
        scratch_shapes=[pltpu.VMEM((2, tq, LANES), F32),
                        pltpu.VMEM((2, tq, LANES), F32),
                        pltpu.VMEM((2, tq, DIFF_V), F32)],
        compiler_params=_params("parallel", "parallel", "arbitrary"),
        name="diff_attention",
    )(slopes, *lam_vecs, qkv, qkv, qkv, z, subln_g.reshape(1, DIFF_V).astype(F32))


def _mem_attn_kernel(q_ref, k_ref, v_ref, z_ref, o_ref):
    scale = MEM_HEAD_DIM ** -0.5
    for h in range(MEM_HEADS):
        sl = slice(h * MEM_HEAD_DIM, (h + 1) * MEM_HEAD_DIM)
        s = lax.dot_general(q_ref[:, sl], k_ref[:, sl], (((1,), (1,)), ((), ())),
                            preferred_element_type=F32) * scale
        m = jnp.max(s, axis=-1, keepdims=True)
        p = jnp.exp(s - m)
        l = jnp.sum(p, axis=-1, keepdims=True)
        o = jnp.dot(p.astype(BF16), v_ref[:, sl], preferred_element_type=F32) / l
        z = z_ref[:, sl].astype(F32)
        o_ref[:, sl] = (o * _silu(z)).astype(o_ref.dtype)


def _mem_attention(qm, kv, z, *, batch, seq, mem_len, tq=512):
    t_total = qm.shape[0]
    n_q = seq // tq
    z_block = D_MIX // D_MEM
    return pl.pallas_call(
        _mem_attn_kernel,
        grid=(batch, n_q),
        in_specs=[
            pl.BlockSpec((tq, D_MEM), lambda b, qi: (b * n_q + qi, 0)),
            pl.BlockSpec((mem_len, D_MEM), lambda b, qi: (b, 0)),
            pl.BlockSpec((mem_len, D_MEM), lambda b, qi: (b, 1)),
            pl.BlockSpec((tq, D_MEM), lambda b, qi: (b * n_q + qi, z_block)),
        ],
        out_specs=pl.BlockSpec((tq, D_MEM), lambda b, qi: (b * n_q + qi, 0)),
        out_shape=jax.ShapeDtypeStruct((t_total, D_MEM), BF16),
        compiler_params=_params("parallel", "arbitrary"),
        name="mem_attention",
    )(qm, kv, kv, z)


def _lambda_init(layer_idx):
    return 0.8 - 0.6 * math.exp(-0.3 * layer_idx)


def kernel(x, mem, l0_norm_g, l0_w_in, l0_pool_w, l0_pool_scale, l0_mem_norm_g,
           l0_w_mem_kv, l0_w_out, l1_norm_g, l1_w_in, l1_lambda_q1, l1_lambda_k1,
           l1_lambda_q2, l1_lambda_k2, l1_subln_g, l1_mem_norm_g, l1_w_mem_kv,
           l1_w_out, final_norm_g):
    batch, seq, d = x.shape
    mem_len = mem.shape[1]
    x2 = x.reshape(batch * seq, d)
    mem2 = mem.reshape(batch * mem_len, d)
    dims = dict(batch=batch, seq=seq)

    h = _rmsnorm(x2, l0_norm_g, BF16)
    u = _matmul(h, l0_w_in, 0, D_MIX)
    qm = _matmul(h, l0_w_in, D_MIX, D_MEM)
    z = _matmul(h, l0_w_in, D_MIX + D_MEM, D_INNER)
    kv = _matmul(_rmsnorm(mem2, l0_mem_norm_g, BF16), l0_w_mem_kv, 0, 2 * D_MEM)
    y_mix = _pool_mixer(u, z, l0_pool_w.astype(BF16), l0_pool_scale, **dims)
    y_mem = _mem_attention(qm, kv, z, mem_len=mem_len, **dims)
    x2 = _out_proj(y_mix, y_mem, l0_w_out.astype(BF16), x2)

    h = _rmsnorm(x2, l1_norm_g, BF16)
    qkv = _matmul(h, l1_w_in, 0, 3 * D_MIX)
    qm = _matmul(h, l1_w_in, 3 * D_MIX, D_MEM)
    z = _matmul(h, l1_w_in, 3 * D_MIX + D_MEM, D_INNER)
    kv = _matmul(_rmsnorm(mem2, l1_mem_norm_g, BF16), l1_w_mem_kv, 0, 2 * D_MEM)
    y_mix = _diff_attention(qkv, z, (l1_lambda_q1, l1_lambda_k1, l1_lambda_q2, l1_lambda_k2),
                            l1_subln_g, lam_init=_lambda_init(1), **dims)
    y_mem = _mem_attention(qm, kv, z, mem_len=mem_len, **dims)
    x2 = _out_proj(y_mix, y_mem, l1_w_out.astype(BF16), x2)

    return _rmsnorm(x2, final_norm_g, F32).reshape(batch, seq, d)
```

```python
import functools
import math

import numpy as np
import jax
import jax.numpy as jnp
from jax import lax
from jax.experimental import pallas as pl
from jax.experimental.pallas import tpu as pltpu

F32 = jnp.float32
BF16 = jnp.bfloat16

D_MODEL = 4096
D_INNER = 2 * D_MODEL
D_MEM = D_INNER // 4
D_MIX = D_INNER - D_MEM
MEM_HEADS = 4
MEM_HEAD_DIM = D_MEM // MEM_HEADS
POOL_WINDOWS = (2, 4, 8, 16)
POOL_GROUP = D_MIX // len(POOL_WINDOWS)
POOL_HALO = 16
DIFF_QK = 128
DIFF_V = 2 * DIFF_QK
DIFF_HEADS = D_MIX // DIFF_V
RMS_EPS = 1e-6
SUBLN_EPS = 1e-5
MASK_VALUE = -1e30

V7X_VMEM_BYTES = 64 * 1024 * 1024
VMEM_LIMIT_BYTES = V7X_VMEM_BYTES - 8 * 1024 * 1024


def _params(*semantics):
    return pltpu.CompilerParams(dimension_semantics=semantics,
                                vmem_limit_bytes=VMEM_LIMIT_BYTES)


def _silu(z):
    return z / (1.0 + jnp.exp(-z))


def _alibi_slopes(n):
    def pow2(m):
        start = 2.0 ** (-8.0 / m)
        return [start ** (i + 1) for i in range(m)]
    if math.log2(n).is_integer():
        s = pow2(n)
    else:
        c = 2 ** math.floor(math.log2(n))
        s = pow2(c) + pow2(2 * c)[0::2][: n - c]
    return np.asarray(s, dtype=np.float32)


def _rmsnorm_kernel(x_ref, g_ref, o_ref, *, eps):
    x = x_ref[...].astype(F32)
    ms = jnp.mean(x * x, axis=-1, keepdims=True)
    o_ref[...] = (x * lax.rsqrt(ms + eps) * g_ref[...]).astype(o_ref.dtype)


def _rmsnorm(x, g, out_dtype, *, eps=RMS_EPS, tm=256):
    m, d = x.shape
    return pl.pallas_call(
        functools.partial(_rmsnorm_kernel, eps=eps),
        grid=(m // tm,),
        in_specs=[pl.BlockSpec((tm, d), lambda i: (i, 0)),
                  pl.BlockSpec((1, d), lambda i: (0, 0))],
        out_specs=pl.BlockSpec((tm, d), lambda i: (i, 0)),
        out_shape=jax.ShapeDtypeStruct((m, d), out_dtype),
        compiler_params=_params("parallel"),
        name="rmsnorm",
    )(x, g.reshape(1, d).astype(F32))


def _matmul_kernel(a_ref, w_ref, o_ref, wb_ref, *, gate):
    @pl.when(pl.program_id(1) == 0)
    def _():
        wb_ref[...] = w_ref[...].astype(BF16)

    acc = jnp.dot(a_ref[...], wb_ref[...], preferred_element_type=F32)
    if gate:
        acc = _silu(acc)
    o_ref[...] = acc.astype(o_ref.dtype)


def _matmul(a, w, col0, ncols, out_dtype=BF16, *, gate=False, bm=512, bn=1024):
    m, k = a.shape
    bm = min(bm, m)
    assert col0 % bn == 0 and ncols % bn == 0 and m % bm == 0
    j0 = col0 // bn
    return pl.pallas_call(
        functools.partial(_matmul_kernel, gate=gate),
        grid=(ncols // bn, m // bm),
        in_specs=[pl.BlockSpec((bm, k), lambda j, i: (i, 0)),
                  pl.BlockSpec((k, bn), lambda j, i: (0, j0 + j))],
        out_specs=pl.BlockSpec((bm, bn), lambda j, i: (i, j)),
        out_shape=jax.ShapeDtypeStruct((m, ncols), out_dtype),
        scratch_shapes=[pltpu.VMEM((k, bn), BF16)],
        compiler_params=_params("parallel", "arbitrary"),
        name="matmul",
    )(a, w)


OUT_BK = D_MEM
OUT_MIX_CHUNKS = D_MIX // OUT_BK


def _out_proj_kernel(ymix_ref, ymem_ref, w_ref, x_ref, o_ref):
    kk = pl.program_id(2)

    @pl.when(kk == 0)
    def _():
        o_ref[...] = x_ref[...] + jnp.dot(ymix_ref[...], w_ref[...],
                                          preferred_element_type=F32)

    @pl.when(jnp.logical_and(kk > 0, kk < OUT_MIX_CHUNKS))
    def _():
        o_ref[...] += jnp.dot(ymix_ref[...], w_ref[...],
                              preferred_element_type=F32)

    @pl.when(kk == OUT_MIX_CHUNKS)
    def _():
        o_ref[...] += jnp.dot(ymem_ref[...], w_ref[...],
                              preferred_element_type=F32)


def _out_proj(ymix, ymem, w, x, *, bm=1024, bn=1024):
    m = x.shape[0]
    n = w.shape[1]
    last = OUT_MIX_CHUNKS - 1
    return pl.pallas_call(
        _out_proj_kernel,
        grid=(m // bm, n // bn, OUT_MIX_CHUNKS + 1),
        in_specs=[
            pl.BlockSpec((bm, OUT_BK), lambda i, j, k: (i, jnp.minimum(k, last))),
            pl.BlockSpec((bm, OUT_BK), lambda i, j, k: (i, 0)),
            pl.BlockSpec((OUT_BK, bn), lambda i, j, k: (k, j)),
            pl.BlockSpec((bm, bn), lambda i, j, k: (i, j)),
        ],
        out_specs=pl.BlockSpec((bm, bn), lambda i, j, k: (i, j)),
        out_shape=jax.ShapeDtypeStruct((m, n), F32),
        compiler_params=_params("parallel", "parallel", "arbitrary"),
        name="out_proj",
    )(ymix, ymem, w, x)


def _pool_kernel(u_ref, prev_ref, gate_ref, w_ref, scale_ref, *rest, ts, tr, window):
    o_ref, wb_ref = rest[-2:]
    si = pl.program_id(1)

    @pl.when(jnp.logical_and(pl.program_id(0) == 0, si == 0))
    def _():
        wb_ref[...] = w_ref[0].astype(BF16)

    def pooled(rc):
        r0 = rc * tr
        if rc == 0:
            prev = jnp.where(si > 0, prev_ref[...].astype(F32), 0.0)
            xs = jnp.concatenate([prev, u_ref[0:tr, :].astype(F32)], axis=0)
        else:
            xs = u_ref[r0 - POOL_HALO:r0 + tr, :].astype(F32)
        s = xs
        k = 1
        while k < window:
            s = s + pltpu.roll(s, k, 0)
            k *= 2
        t = si * ts + r0 + lax.broadcasted_iota(jnp.int32, (tr, 1), 0)
        inv_cnt = 1.0 / jnp.minimum(t + 1, window).astype(F32)
        return (s[POOL_HALO:] * inv_cnt - xs[POOL_HALO:]).astype(BF16)

    n_sub = ts // tr
    nxt = pooled(0)
    for rc in range(n_sub):
        cur = nxt
        if rc + 1 < n_sub:
            nxt = pooled(rc + 1)
        rows = pl.ds(rc * tr, tr)
        mixed = jnp.dot(cur, wb_ref[...], preferred_element_type=F32)
        o_ref[rows, :] = (mixed * scale_ref[...] * gate_ref[rows, :].astype(F32)
                          ).astype(o_ref.dtype)


def _pool_mixer(u, gate, pool_w, pool_scale, *, batch, seq, ts=512, tr=256):
    t_total = u.shape[0]
    c = POOL_GROUP
    n_s = seq // ts
    halo_blocks_per_tile = ts // POOL_HALO
    scale = pool_scale.reshape(1, D_MIX).astype(F32)

    out = None
    for g, window in enumerate(POOL_WINDOWS):
        def prev_map(b, si):
            first = (b * n_s + si) * halo_blocks_per_tile
            return (jnp.maximum(first - 1, 0), g)

        tile = pl.BlockSpec((ts, c), lambda b, si: (b * n_s + si, g))
        in_specs = [tile, pl.BlockSpec((POOL_HALO, c), prev_map), tile,
                    pl.BlockSpec((1, c, c), lambda b, si: (g, 0, 0)),
                    pl.BlockSpec((1, c), lambda b, si: (0, g))]
        args = [u, u, gate, pool_w, scale]
        if out is not None:
            in_specs.append(pl.BlockSpec(memory_space=pl.ANY))
            args.append(out)
        out = pl.pallas_call(
            functools.partial(_pool_kernel, ts=ts, tr=tr, window=window),
            grid=(batch, n_s),
            in_specs=in_specs,
            out_specs=tile,
            out_shape=jax.ShapeDtypeStruct((t_total, D_MIX), BF16),
            scratch_shapes=[pltpu.VMEM((c, c), BF16)],
            input_output_aliases={} if g == 0 else {len(args) - 1: 0},
            compiler_params=_params("arbitrary", "arbitrary"),
            name=f"pool_mixer_w{window}",
        )(*args)
    return out


def _diff_attn_kernel(slope_ref, lq1_ref, lk1_ref, lq2_ref, lk2_ref,
                      q_ref, k_ref, v_ref, gate_ref, g_ref, o_ref,
                      *, tq, tr, n_q, lam_init):
    h = pl.program_id(1)
    qi = pl.program_id(2)
    slope = slope_ref[h]
    scale = DIFF_QK ** -0.5

    f = lambda ref: ref[...].astype(F32)
    lam = (jnp.exp(jnp.sum(f(lq1_ref) * f(lk1_ref), keepdims=True))
           - jnp.exp(jnp.sum(f(lq2_ref) * f(lk2_ref), keepdims=True)) + lam_init)

    def logits(j, r0, n_keys):
        sl = slice(j * DIFF_QK, (j + 1) * DIFF_QK)
        return lax.dot_general(q_ref[pl.ds(r0, tr), sl], k_ref[0:n_keys, sl],
                               (((1,), (1,)), ((), ())), preferred_element_type=F32)

    def softmax(s, n_keys):
        rel = lax.broadcasted_iota(jnp.int32, (1, n_keys), 1) - (n_keys - tr)
        s = s * scale + slope * rel.astype(F32)
        n_past = n_keys - tr
        causal = rel[:, n_past:] <= lax.broadcasted_iota(jnp.int32, (tr, 1), 0)
        own = jnp.where(causal, s[:, n_past:], MASK_VALUE)
        m = jnp.max(own, axis=-1, keepdims=True)
        if n_past:
            past = s[:, :n_past]
            m = jnp.maximum(m, jnp.max(past, axis=-1, keepdims=True))
        p = jnp.exp(own - m)
        if n_past:
            p = jnp.concatenate([jnp.exp(past - m), p], axis=-1)
        return p.astype(BF16), jnp.sum(p, axis=-1, keepdims=True)

    def finish(r0, outs):
        o = outs[0] - lam * outs[1]
        ms = jnp.mean(o * o, axis=-1, keepdims=True)
        y = (o * lax.rsqrt(ms + SUBLN_EPS) * g_ref[...]) * (1.0 - lam_init)
        gate = gate_ref[pl.ds(r0, tr), :].astype(F32)
        o_ref[pl.ds(r0, tr), :] = (y * gate).astype(o_ref.dtype)

    for c in range(n_q):
        @pl.when(qi == c)
        def _(c=c):
            chains = [(j, r * tr, c * tq + (r + 1) * tr)
                      for r in range(tq // tr) for j in range(2)]
            s_next = logits(*chains[0])
            outs = []
            for n, (j, r0, n_keys) in enumerate(chains):
                s = s_next
                if n + 1 < len(chains):
                    s_next = logits(*chains[n + 1])
                p, l = softmax(s, n_keys)
                outs.append(jnp.dot(p, v_ref[0:n_keys, :], preferred_element_type=F32) / l)
                if j == 1:
                    finish(r0, outs)
                    outs = []


def _diff_attention(qkv, gate, lam_vecs, subln_g, *, batch, seq, lam_init, tq=512, tr=128):
    t_total = qkv.shape[0]
    n_q = seq // tq
    slopes = jnp.asarray(_alibi_slopes(DIFF_HEADS))
    vec_spec = pl.BlockSpec((1, DIFF_QK), lambda b, h, qi: (0, 0))
    lam_vecs = [v.reshape(1, DIFF_QK).astype(F32) for v in lam_vecs]
    return pl.pallas_call(
        functools.partial(_diff_attn_kernel, tq=tq, tr=tr, n_q=n_q, lam_init=lam_init),
        grid=(batch, DIFF_HEADS, n_q),
        in_specs=[
            pl.BlockSpec(memory_space=pltpu.SMEM),
            vec_spec, vec_spec, vec_spec, vec_spec,
            pl.BlockSpec((tq, DIFF_V), lambda b, h, qi: (b * n_q + qi, h)),
            pl.BlockSpec((seq, DIFF_V), lambda b, h, qi: (b, DIFF_HEADS + h)),
            pl.BlockSpec((seq, DIFF_V), lambda b, h, qi: (b, 2 * DIFF_HEADS + h)),
            pl.BlockSpec((tq, DIFF_V), lambda b, h, qi: (b * n_q + qi, h)),
            pl.BlockSpec((1, DIFF_V), lambda b, h, qi: (0, 0)),
        ],
        out_specs=pl.BlockSpec((tq, DIFF_V), lambda b, h, qi: (b * n_q + qi, h)),
        out_shape=jax.ShapeDtypeStruct((t_total, D_MIX), BF16),
        compiler_params=_params("parallel", "parallel", "arbitrary"),
        name="diff_attention",
    )(slopes, *lam_vecs, qkv, qkv, qkv, gate, subln_g.reshape(1, DIFF_V).astype(F32))


def _mem_attn_kernel(q_ref, k_ref, v_ref, gate_ref, o_ref):
    scale = MEM_HEAD_DIM ** -0.5
    for h in range(MEM_HEADS):
        sl = slice(h * MEM_HEAD_DIM, (h + 1) * MEM_HEAD_DIM)
        s = lax.dot_general(q_ref[:, sl], k_ref[:, sl], (((1,), (1,)), ((), ())),
                            preferred_element_type=F32) * scale
        m = jnp.max(s, axis=-1, keepdims=True)
        p = jnp.exp(s - m)
        l = jnp.sum(p, axis=-1, keepdims=True)
        o = jnp.dot(p.astype(BF16), v_ref[:, sl], preferred_element_type=F32) / l
        o_ref[:, sl] = (o * gate_ref[:, sl].astype(F32)).astype(o_ref.dtype)


def _mem_attention(qm, kv, gate, *, batch, seq, mem_len, tq=512):
    t_total = qm.shape[0]
    n_q = seq // tq
    z_block = D_MIX // D_MEM
    return pl.pallas_call(
        _mem_attn_kernel,
        grid=(batch, n_q),
        in_specs=[
            pl.BlockSpec((tq, D_MEM), lambda b, qi: (b * n_q + qi, 0)),
            pl.BlockSpec((mem_len, D_MEM), lambda b, qi: (b, 0)),
            pl.BlockSpec((mem_len, D_MEM), lambda b, qi: (b, 1)),
            pl.BlockSpec((tq, D_MEM), lambda b, qi: (b * n_q + qi, z_block)),
        ],
        out_specs=pl.BlockSpec((tq, D_MEM), lambda b, qi: (b * n_q + qi, 0)),
        out_shape=jax.ShapeDtypeStruct((t_total, D_MEM), BF16),
        compiler_params=_params("parallel", "arbitrary"),
        name="mem_attention",
    )(qm, kv, kv, gate)


def _lambda_init(layer_idx):
    return 0.8 - 0.6 * math.exp(-0.3 * layer_idx)


def kernel(x, mem, l0_norm_g, l0_w_in, l0_pool_w, l0_pool_scale, l0_mem_norm_g,
           l0_w_mem_kv, l0_w_out, l1_norm_g, l1_w_in, l1_lambda_q1, l1_lambda_k1,
           l1_lambda_q2, l1_lambda_k2, l1_subln_g, l1_mem_norm_g, l1_w_mem_kv,
           l1_w_out, final_norm_g):
    batch, seq, d = x.shape
    mem_len = mem.shape[1]
    x2 = x.reshape(batch * seq, d)
    mem2 = mem.reshape(batch * mem_len, d)
    dims = dict(batch=batch, seq=seq)

    h = _rmsnorm(x2, l0_norm_g, BF16)
    u = _matmul(h, l0_w_in, 0, D_MIX)
    qm = _matmul(h, l0_w_in, D_MIX, D_MEM)
    gate = _matmul(h, l0_w_in, D_MIX + D_MEM, D_INNER, gate=True)
    kv = _matmul(_rmsnorm(mem2, l0_mem_norm_g, BF16), l0_w_mem_kv, 0, 2 * D_MEM)
    y_mix = _pool_mixer(u, gate, l0_pool_w, l0_pool_scale, **dims)
    y_mem = _mem_attention(qm, kv, gate, mem_len=mem_len, **dims)
    x2 = _out_proj(y_mix, y_mem, l0_w_out.astype(BF16), x2)

    h = _rmsnorm(x2, l1_norm_g, BF16)
    qkv = _matmul(h, l1_w_in, 0, 3 * D_MIX)
    qm = _matmul(h, l1_w_in, 3 * D_MIX, D_MEM)
    gate = _matmul(h, l1_w_in, 3 * D_MIX + D_MEM, D_INNER, gate=True)
    kv = _matmul(_rmsnorm(mem2, l1_mem_norm_g, BF16), l1_w_mem_kv, 0, 2 * D_MEM)
    y_mix = _diff_attention(qkv, gate, (l1_lambda_q1, l1_lambda_k1, l1_lambda_q2, l1_lambda_k2),
                            l1_subln_g, lam_init=_lambda_init(1), **dims)
    y_mem = _mem_attention(qm, kv, gate, mem_len=mem_len, **dims)
    x2 = _out_proj(y_mix, y_mem, l1_w_out.astype(BF16), x2)

    return _rmsnorm(x2, final_norm_g, F32).reshape(batch, seq, d)
```

```python
import functools
import math

import numpy as np
import jax
import jax.numpy as jnp
from jax import lax
from jax.experimental import pallas as pl
from jax.experimental.pallas import tpu as pltpu

F32 = jnp.float32
BF16 = jnp.bfloat16

D_MODEL = 4096
D_INNER = 2 * D_MODEL
D_MEM = D_INNER // 4
D_MIX = D_INNER - D_MEM
MEM_HEADS = 4
MEM_HEAD_DIM = D_MEM // MEM_HEADS
POOL_WINDOWS = (2, 4, 8, 16)
POOL_GROUP = D_MIX // len(POOL_WINDOWS)
POOL_HALO = 16
DIFF_QK = 128
DIFF_V = 2 * DIFF_QK
DIFF_HEADS = D_MIX // DIFF_V
RMS_EPS = 1e-6
SUBLN_EPS = 1e-5
MASK_VALUE = -1e30

V7X_VMEM_BYTES = 64 * 1024 * 1024
VMEM_LIMIT_BYTES = V7X_VMEM_BYTES - 8 * 1024 * 1024


def _params(*semantics):
    return pltpu.CompilerParams(dimension_semantics=semantics,
                                vmem_limit_bytes=VMEM_LIMIT_BYTES)


def _silu(z):
    return z / (1.0 + jnp.exp(-z))


def _alibi_slopes(n):
    def pow2(m):
        start = 2.0 ** (-8.0 / m)
        return [start ** (i + 1) for i in range(m)]
    if math.log2(n).is_integer():
        s = pow2(n)
    else:
        c = 2 ** math.floor(math.log2(n))
        s = pow2(c) + pow2(2 * c)[0::2][: n - c]
    return np.asarray(s, dtype=np.float32)


def _rmsnorm_kernel(x_ref, g_ref, o_ref, *, eps):
    x = x_ref[...].astype(F32)
    ms = jnp.mean(x * x, axis=-1, keepdims=True)
    o_ref[...] = (x * lax.rsqrt(ms + eps) * g_ref[...]).astype(o_ref.dtype)


def _rmsnorm(x, g, out_dtype, *, eps=RMS_EPS, tm=256):
    m, d = x.shape
    return pl.pallas_call(
        functools.partial(_rmsnorm_kernel, eps=eps),
        grid=(m // tm,),
        in_specs=[pl.BlockSpec((tm, d), lambda i: (i, 0)),
                  pl.BlockSpec((1, d), lambda i: (0, 0))],
        out_specs=pl.BlockSpec((tm, d), lambda i: (i, 0)),
        out_shape=jax.ShapeDtypeStruct((m, d), out_dtype),
        compiler_params=_params("parallel"),
        name="rmsnorm",
    )(x, g.reshape(1, d).astype(F32))


def _matmul_kernel(a_ref, w_hbm, o_ref, stage_ref, wb_ref, sem, *, gate, col0, bn, n_panels):
    j = pl.program_id(0)

    def panel_copy(jj):
        return pltpu.make_async_copy(w_hbm.at[:, pl.ds(col0 + jj * bn, bn)], stage_ref, sem)

    @pl.when(pl.program_id(1) == 0)
    def _():
        @pl.when(j == 0)
        def _():
            panel_copy(0).start()

        panel_copy(j).wait()
        wb_ref[...] = stage_ref[...].astype(BF16)

        @pl.when(j + 1 < n_panels)
        def _():
            panel_copy(j + 1).start()

    acc = jnp.dot(a_ref[...], wb_ref[...], preferred_element_type=F32)
    if gate:
        acc = _silu(acc)
    o_ref[...] = acc.astype(o_ref.dtype)


def _matmul(a, w, col0, ncols, out_dtype=BF16, *, gate=False, bm=1024, bn=1024):
    m, k = a.shape
    bm = min(bm, m)
    assert ncols % bn == 0 and m % bm == 0
    n_panels = ncols // bn
    return pl.pallas_call(
        functools.partial(_matmul_kernel, gate=gate, col0=col0, bn=bn, n_panels=n_panels),
        grid=(n_panels, m // bm),
        in_specs=[pl.BlockSpec((bm, k), lambda j, i: (i, 0)),
                  pl.BlockSpec(memory_space=pl.ANY)],
        out_specs=pl.BlockSpec((bm, bn), lambda j, i: (i, j)),
        out_shape=jax.ShapeDtypeStruct((m, ncols), out_dtype),
        scratch_shapes=[pltpu.VMEM((k, bn), F32),
                        pltpu.VMEM((k, bn), BF16),
                        pltpu.SemaphoreType.DMA(())],
        compiler_params=_params("arbitrary", "arbitrary"),
        name="matmul",
    )(a, w)


OUT_BK = D_MEM
OUT_MIX_CHUNKS = D_MIX // OUT_BK


def _out_proj_kernel(ymix_ref, ymem_ref, w_ref, x_ref, o_ref):
    kk = pl.program_id(2)

    @pl.when(kk == 0)
    def _():
        o_ref[...] = x_ref[...] + jnp.dot(ymix_ref[...], w_ref[...],
                                          preferred_element_type=F32)

    @pl.when(jnp.logical_and(kk > 0, kk < OUT_MIX_CHUNKS))
    def _():
        o_ref[...] += jnp.dot(ymix_ref[...], w_ref[...],
                              preferred_element_type=F32)

    @pl.when(kk == OUT_MIX_CHUNKS)
    def _():
        o_ref[...] += jnp.dot(ymem_ref[...], w_ref[...],
                              preferred_element_type=F32)


def _out_proj(ymix, ymem, w, x, *, bm=1024, bn=1024):
    m = x.shape[0]
    n = w.shape[1]
    last = OUT_MIX_CHUNKS - 1
    return pl.pallas_call(
        _out_proj_kernel,
        grid=(m // bm, n // bn, OUT_MIX_CHUNKS + 1),
        in_specs=[
            pl.BlockSpec((bm, OUT_BK), lambda i, j, k: (i, jnp.minimum(k, last))),
            pl.BlockSpec((bm, OUT_BK), lambda i, j, k: (i, 0)),
            pl.BlockSpec((OUT_BK, bn), lambda i, j, k: (k, j)),
            pl.BlockSpec((bm, bn), lambda i, j, k: (i, j)),
        ],
        out_specs=pl.BlockSpec((bm, bn), lambda i, j, k: (i, j)),
        out_shape=jax.ShapeDtypeStruct((m, n), F32),
        compiler_params=_params("parallel", "parallel", "arbitrary"),
        name="out_proj",
    )(ymix, ymem, w, x)


def _pool_kernel(u_ref, prev_ref, gate_ref, w_ref, scale_ref, o_ref, wb_ref, *, ts, tr):
    g = pl.program_id(0)
    si = pl.program_id(2)

    @pl.when(jnp.logical_and(pl.program_id(1) == 0, si == 0))
    def _():
        wb_ref[...] = w_ref[0].astype(BF16)

    for idx, window in enumerate(POOL_WINDOWS):
        @pl.when(g == idx)
        def _(window=window):
            _pool_tile(u_ref, prev_ref, gate_ref, scale_ref, o_ref, wb_ref, si,
                       ts=ts, tr=tr, window=window)


def _pool_tile(u_ref, prev_ref, gate_ref, scale_ref, o_ref, wb_ref, si, *, ts, tr, window):
    def pooled(rc):
        r0 = rc * tr
        if rc == 0:
            prev = jnp.where(si > 0, prev_ref[...].astype(F32), 0.0)
            xs = jnp.concatenate([prev, u_ref[0:tr, :].astype(F32)], axis=0)
        else:
            xs = u_ref[r0 - POOL_HALO:r0 + tr, :].astype(F32)
        s = xs
        k = 1
        while k < window:
            s = s + pltpu.roll(s, k, 0)
            k *= 2
        t = si * ts + r0 + lax.broadcasted_iota(jnp.int32, (tr, 1), 0)
        inv_cnt = 1.0 / jnp.minimum(t + 1, window).astype(F32)
        return (s[POOL_HALO:] * inv_cnt - xs[POOL_HALO:]).astype(BF16)

    n_sub = ts // tr
    nxt = pooled(0)
    for rc in range(n_sub):
        cur = nxt
        if rc + 1 < n_sub:
            nxt = pooled(rc + 1)
        rows = pl.ds(rc * tr, tr)
        mixed = jnp.dot(cur, wb_ref[...], preferred_element_type=F32)
        o_ref[rows, :] = (mixed * scale_ref[...] * gate_ref[rows, :].astype(F32)
                          ).astype(o_ref.dtype)


def _pool_mixer(u, gate, pool_w, pool_scale, *, batch, seq, ts=512, tr=256):
    t_total = u.shape[0]
    c = POOL_GROUP
    n_s = seq // ts
    halo_blocks_per_tile = ts // POOL_HALO

    def prev_map(g, b, si):
        first = (b * n_s + si) * halo_blocks_per_tile
        return (jnp.maximum(first - 1, 0), g)

    tile = pl.BlockSpec((ts, c), lambda g, b, si: (b * n_s + si, g))
    return pl.pallas_call(
        functools.partial(_pool_kernel, ts=ts, tr=tr),
        grid=(len(POOL_WINDOWS), batch, n_s),
        in_specs=[tile, pl.BlockSpec((POOL_HALO, c), prev_map), tile,
                  pl.BlockSpec((1, c, c), lambda g, b, si: (g, 0, 0)),
                  pl.BlockSpec((1, c), lambda g, b, si: (0, g))],
        out_specs=tile,
        out_shape=jax.ShapeDtypeStruct((t_total, D_MIX), BF16),
        scratch_shapes=[pltpu.VMEM((c, c), BF16)],
        compiler_params=_params("arbitrary", "arbitrary", "arbitrary"),
        name="pool_mixer",
    )(u, u, gate, pool_w, pool_scale.reshape(1, D_MIX).astype(F32))


def _diff_attn_kernel(slope_ref, lq1_ref, lk1_ref, lq2_ref, lk2_ref,
                      q_ref, k_ref, v_ref, gate_ref, g_ref, o_ref,
                      *, tq, tr, n_q, lam_init):
    h = pl.program_id(1)
    qi = pl.program_id(2)
    slope = slope_ref[h]
    scale = DIFF_QK ** -0.5

    f = lambda ref: ref[...].astype(F32)
    lam = (jnp.exp(jnp.sum(f(lq1_ref) * f(lk1_ref), keepdims=True))
           - jnp.exp(jnp.sum(f(lq2_ref) * f(lk2_ref), keepdims=True)) + lam_init)

    def logits(j, r0, n_keys):
        sl = slice(j * DIFF_QK, (j + 1) * DIFF_QK)
        return lax.dot_general(q_ref[pl.ds(r0, tr), sl], k_ref[0:n_keys, sl],
                               (((1,), (1,)), ((), ())), preferred_element_type=F32)

    def softmax(s, n_keys):
        rel = lax.broadcasted_iota(jnp.int32, (1, n_keys), 1) - (n_keys - tr)
        s = s * scale + slope * rel.astype(F32)
        n_past = n_keys - tr
        causal = rel[:, n_past:] <= lax.broadcasted_iota(jnp.int32, (tr, 1), 0)
        own = jnp.where(causal, s[:, n_past:], MASK_VALUE)
        m = jnp.max(own, axis=-1, keepdims=True)
        if n_past:
            past = s[:, :n_past]
            m = jnp.maximum(m, jnp.max(past, axis=-1, keepdims=True))
        p = jnp.exp(own - m)
        if n_past:
            p = jnp.concatenate([jnp.exp(past - m), p], axis=-1)
        return p.astype(BF16), jnp.sum(p, axis=-1, keepdims=True)

    def finish(r0, outs):
        o = outs[0] - lam * outs[1]
        ms = jnp.mean(o * o, axis=-1, keepdims=True)
        y = (o * lax.rsqrt(ms + SUBLN_EPS) * g_ref[...]) * (1.0 - lam_init)
        gate = gate_ref[pl.ds(r0, tr), :].astype(F32)
        o_ref[pl.ds(r0, tr), :] = (y * gate).astype(o_ref.dtype)

    for c in range(n_q):
        @pl.when(qi == c)
        def _(c=c):
            chains = [(j, r * tr, c * tq + (r + 1) * tr)
                      for r in range(tq // tr) for j in range(2)]
            s_next = logits(*chains[0])
            outs = []
            for n, (j, r0, n_keys) in enumerate(chains):
                s = s_next
                if n + 1 < len(chains):
                    s_next = logits(*chains[n + 1])
                p, l = softmax(s, n_keys)
                outs.append(jnp.dot(p, v_ref[0:n_keys, :], preferred_element_type=F32) / l)
                if j == 1:
                    finish(r0, outs)
                    outs = []


def _diff_attention(qkv, gate, lam_vecs, subln_g, *, batch, seq, lam_init, tq=512, tr=256):
    t_total = qkv.shape[0]
    n_q = seq // tq
    slopes = jnp.asarray(_alibi_slopes(DIFF_HEADS))
    vec_spec = pl.BlockSpec((1, DIFF_QK), lambda b, h, qi: (0, 0))
    lam_vecs = [v.reshape(1, DIFF_QK).astype(F32) for v in lam_vecs]
    return pl.pallas_call(
        functools.partial(_diff_attn_kernel, tq=tq, tr=tr, n_q=n_q, lam_init=lam_init),
        grid=(batch, DIFF_HEADS, n_q),
        in_specs=[
            pl.BlockSpec(memory_space=pltpu.SMEM),
            vec_spec, vec_spec, vec_spec, vec_spec,
            pl.BlockSpec((tq, DIFF_V), lambda b, h, qi: (b * n_q + qi, h)),
            pl.BlockSpec((seq, DIFF_V), lambda b, h, qi: (b, DIFF_HEADS + h)),
            pl.BlockSpec((seq, DIFF_V), lambda b, h, qi: (b, 2 * DIFF_HEADS + h)),
            pl.BlockSpec((tq, DIFF_V), lambda b, h, qi: (b * n_q + qi, h)),
            pl.BlockSpec((1, DIFF_V), lambda b, h, qi: (0, 0)),
        ],
        out_specs=pl.BlockSpec((tq, DIFF_V), lambda b, h, qi: (b * n_q + qi, h)),
        out_shape=jax.ShapeDtypeStruct((t_total, D_MIX), BF16),
        compiler_params=_params("parallel", "parallel", "arbitrary"),
        name="diff_attention",
    )(slopes, *lam_vecs, qkv, qkv, qkv, gate, subln_g.reshape(1, DIFF_V).astype(F32))


def _mem_attn_kernel(q_ref, k_ref, v_ref, gate_ref, o_ref):
    scale = MEM_HEAD_DIM ** -0.5
    for h in range(MEM_HEADS):
        sl = slice(h * MEM_HEAD_DIM, (h + 1) * MEM_HEAD_DIM)
        s = lax.dot_general(q_ref[:, sl], k_ref[:, sl], (((1,), (1,)), ((), ())),
                            preferred_element_type=F32) * scale
        m = jnp.max(s, axis=-1, keepdims=True)
        p = jnp.exp(s - m)
        l = jnp.sum(p, axis=-1, keepdims=True)
        o = jnp.dot(p.astype(BF16), v_ref[:, sl], preferred_element_type=F32) / l
        o_ref[:, sl] = (o * gate_ref[:, sl].astype(F32)).astype(o_ref.dtype)


def _mem_attention(qm, kv, gate, *, batch, seq, mem_len, tq=512):
    t_total = qm.shape[0]
    n_q = seq // tq
    z_block = D_MIX // D_MEM
    return pl.pallas_call(
        _mem_attn_kernel,
        grid=(batch, n_q),
        in_specs=[
            pl.BlockSpec((tq, D_MEM), lambda b, qi: (b * n_q + qi, 0)),
            pl.BlockSpec((mem_len, D_MEM), lambda b, qi: (b, 0)),
            pl.BlockSpec((mem_len, D_MEM), lambda b, qi: (b, 1)),
            pl.BlockSpec((tq, D_MEM), lambda b, qi: (b * n_q + qi, z_block)),
        ],
        out_specs=pl.BlockSpec((tq, D_MEM), lambda b, qi: (b * n_q + qi, 0)),
        out_shape=jax.ShapeDtypeStruct((t_total, D_MEM), BF16),
        compiler_params=_params("parallel", "arbitrary"),
        name="mem_attention",
    )(qm, kv, kv, gate)


def _lambda_init(layer_idx):
    return 0.8 - 0.6 * math.exp(-0.3 * layer_idx)


def kernel(x, mem, l0_norm_g, l0_w_in, l0_pool_w, l0_pool_scale, l0_mem_norm_g,
           l0_w_mem_kv, l0_w_out, l1_norm_g, l1_w_in, l1_lambda_q1, l1_lambda_k1,
           l1_lambda_q2, l1_lambda_k2, l1_subln_g, l1_mem_norm_g, l1_w_mem_kv,
           l1_w_out, final_norm_g):
    batch, seq, d = x.shape
    mem_len = mem.shape[1]
    x2 = x.reshape(batch * seq, d)
    mem2 = mem.reshape(batch * mem_len, d)
    dims = dict(batch=batch, seq=seq)

    h = _rmsnorm(x2, l0_norm_g, BF16)
    u = _matmul(h, l0_w_in, 0, D_MIX)
    qm = _matmul(h, l0_w_in, D_MIX, D_MEM)
    gate = _matmul(h, l0_w_in, D_MIX + D_MEM, D_INNER, gate=True)
    kv = _matmul(_rmsnorm(mem2, l0_mem_norm_g, BF16), l0_w_mem_kv, 0, 2 * D_MEM)
    y_mix = _pool_mixer(u, gate, l0_pool_w, l0_pool_scale, **dims)
    y_mem = _mem_attention(qm, kv, gate, mem_len=mem_len, **dims)
    x2 = _out_proj(y_mix, y_mem, l0_w_out.astype(BF16), x2)

    h = _rmsnorm(x2, l1_norm_g, BF16)
    qkv = _matmul(h, l1_w_in, 0, 3 * D_MIX)
    qm = _matmul(h, l1_w_in, 3 * D_MIX, D_MEM)
    gate = _matmul(h, l1_w_in, 3 * D_MIX + D_MEM, D_INNER, gate=True)
    kv = _matmul(_rmsnorm(mem2, l1_mem_norm_g, BF16), l1_w_mem_kv, 0, 2 * D_MEM)
    y_mix = _diff_attention(qkv, gate, (l1_lambda_q1, l1_lambda_k1, l1_lambda_q2, l1_lambda_k2),
                            l1_subln_g, lam_init=_lambda_init(1), **dims)
    y_mem = _mem_attention(qm, kv, gate, mem_len=mem_len, **dims)
    x2 = _out_proj(y_mix, y_mem, l1_w_out.astype(BF16), x2)

    return _rmsnorm(x2, final_norm_g, F32).reshape(batch, seq, d)
```

```python
import functools
import math

import numpy as np
import jax
import jax.numpy as jnp
from jax import lax
from jax.experimental import pallas as pl
from jax.experimental.pallas import tpu as pltpu

F32 = jnp.float32
BF16 = jnp.bfloat16

D_MODEL = 4096
D_INNER = 2 * D_MODEL
D_MEM = D_INNER // 4
D_MIX = D_INNER - D_MEM
MEM_HEADS = 4
MEM_HEAD_DIM = D_MEM // MEM_HEADS
POOL_WINDOWS = (2, 4, 8, 16)
POOL_GROUP = D_MIX // len(POOL_WINDOWS)
POOL_HALO = 16
DIFF_QK = 128
DIFF_V = 2 * DIFF_QK
DIFF_HEADS = D_MIX // DIFF_V
RMS_EPS = 1e-6
SUBLN_EPS = 1e-5
MASK_VALUE = -1e30

V7X_VMEM_BYTES = 64 * 1024 * 1024
VMEM_LIMIT_BYTES = V7X_VMEM_BYTES - 8 * 1024 * 1024


def _params(*semantics):
    return pltpu.CompilerParams(dimension_semantics=semantics,
                                vmem_limit_bytes=VMEM_LIMIT_BYTES)


def _silu(z):
    half = 0.5 * z
    return half + half * jnp.tanh(half)


def _alibi_slopes(n):
    def pow2(m):
        start = 2.0 ** (-8.0 / m)
        return [start ** (i + 1) for i in range(m)]
    if math.log2(n).is_integer():
        s = pow2(n)
    else:
        c = 2 ** math.floor(math.log2(n))
        s = pow2(c) + pow2(2 * c)[0::2][: n - c]
    return np.asarray(s, dtype=np.float32)


POS_RADIX = 64
SLOPE_PIECES = 3


def _alibi_tables(seq):
    rest = _alibi_slopes(DIFF_HEADS).astype(np.float64)
    pieces = []
    for _ in range(SLOPE_PIECES):
        piece = rest.astype(BF16).astype(np.float64)
        pieces.append(piece)
        rest = rest - piece
    assert not rest.any(), "slope pieces must be exact"
    n = SLOPE_PIECES
    slope_cols = np.zeros((DIFF_HEADS, 1, DIFF_QK), np.float64)
    slope_cols[:, 0, 0:n] = POS_RADIX * np.stack(pieces, axis=-1)
    slope_cols[:, 0, n:2 * n] = np.stack(pieces, axis=-1)
    pos = np.arange(seq)
    pos_cols = np.zeros((seq, DIFF_QK), np.float64)
    pos_cols[:, 0:n] = (pos // POS_RADIX)[:, None]
    pos_cols[:, n:2 * n] = (pos % POS_RADIX)[:, None]
    return jnp.asarray(slope_cols, BF16), jnp.asarray(pos_cols, BF16)


def _rmsnorm_kernel(x_ref, g_ref, o_ref, *, eps):
    x = x_ref[...].astype(F32)
    ms = jnp.mean(x * x, axis=-1, keepdims=True)
    o_ref[...] = (x * lax.rsqrt(ms + eps) * g_ref[...]).astype(o_ref.dtype)


def _rmsnorm(x, g, out_dtype, *, eps=RMS_EPS, tm=256):
    m, d = x.shape
    return pl.pallas_call(
        functools.partial(_rmsnorm_kernel, eps=eps),
        grid=(m // tm,),
        in_specs=[pl.BlockSpec((tm, d), lambda i: (i, 0)),
                  pl.BlockSpec((1, d), lambda i: (0, 0))],
        out_specs=pl.BlockSpec((tm, d), lambda i: (i, 0)),
        out_shape=jax.ShapeDtypeStruct((m, d), out_dtype),
        compiler_params=_params("parallel"),
        name="rmsnorm",
    )(x, g.reshape(1, d).astype(F32))


def _matmul_kernel(a_ref, w_hbm, o_ref, stage_ref, wb_ref, sem, *, gate, col0, bn, n_panels,
                   scaled_cols, col_scale):
    j = pl.program_id(0)

    def panel_copy(jj):
        return pltpu.make_async_copy(w_hbm.at[:, pl.ds(col0 + jj * bn, bn)], stage_ref, sem)

    @pl.when(pl.program_id(1) == 0)
    def _():
        @pl.when(j == 0)
        def _():
            panel_copy(0).start()

        panel_copy(j).wait()
        wb_ref[...] = stage_ref[...].astype(BF16)

        @pl.when(j + 1 < n_panels)
        def _():
            panel_copy(j + 1).start()

    acc = jnp.dot(a_ref[...], wb_ref[...], preferred_element_type=F32)
    if gate:
        acc = _silu(acc)
    if scaled_cols:
        acc = acc * jnp.where(j < scaled_cols // bn, col_scale, 1.0)
    o_ref[...] = acc.astype(o_ref.dtype)


def _matmul(a, w, col0, ncols, out_dtype=BF16, *, gate=False, scaled_cols=0, col_scale=1.0,
            bm=1024, bn=1024):
    m, k = a.shape
    bm = min(bm, m)
    assert ncols % bn == 0 and m % bm == 0 and scaled_cols % bn == 0
    n_panels = ncols // bn
    return pl.pallas_call(
        functools.partial(_matmul_kernel, gate=gate, col0=col0, bn=bn, n_panels=n_panels,
                          scaled_cols=scaled_cols, col_scale=col_scale),
        grid=(n_panels, m // bm),
        in_specs=[pl.BlockSpec((bm, k), lambda j, i: (i, 0)),
                  pl.BlockSpec(memory_space=pl.ANY)],
        out_specs=pl.BlockSpec((bm, bn), lambda j, i: (i, j)),
        out_shape=jax.ShapeDtypeStruct((m, ncols), out_dtype),
        scratch_shapes=[pltpu.VMEM((k, bn), F32),
                        pltpu.VMEM((k, bn), BF16),
                        pltpu.SemaphoreType.DMA(())],
        compiler_params=_params("arbitrary", "arbitrary"),
        name="matmul",
    )(a, w)


OUT_BK = D_MEM
OUT_MIX_CHUNKS = D_MIX // OUT_BK


def _out_proj_kernel(ymix_ref, ymem_ref, w_ref, x_ref, o_ref):
    kk = pl.program_id(2)

    @pl.when(kk == 0)
    def _():
        o_ref[...] = x_ref[...] + jnp.dot(ymix_ref[...], w_ref[...],
                                          preferred_element_type=F32)

    @pl.when(jnp.logical_and(kk > 0, kk < OUT_MIX_CHUNKS))
    def _():
        o_ref[...] += jnp.dot(ymix_ref[...], w_ref[...],
                              preferred_element_type=F32)

    @pl.when(kk == OUT_MIX_CHUNKS)
    def _():
        o_ref[...] += jnp.dot(ymem_ref[...], w_ref[...],
                              preferred_element_type=F32)


def _out_proj(ymix, ymem, w, x, *, bm=1024, bn=1024):
    m = x.shape[0]
    n = w.shape[1]
    last = OUT_MIX_CHUNKS - 1
    return pl.pallas_call(
        _out_proj_kernel,
        grid=(m // bm, n // bn, OUT_MIX_CHUNKS + 1),
        in_specs=[
            pl.BlockSpec((bm, OUT_BK), lambda i, j, k: (i, jnp.minimum(k, last))),
            pl.BlockSpec((bm, OUT_BK), lambda i, j, k: (i, 0)),
            pl.BlockSpec((OUT_BK, bn), lambda i, j, k: (k, j)),
            pl.BlockSpec((bm, bn), lambda i, j, k: (i, j)),
        ],
        out_specs=pl.BlockSpec((bm, bn), lambda i, j, k: (i, j)),
        out_shape=jax.ShapeDtypeStruct((m, n), F32),
        compiler_params=_params("parallel", "parallel", "arbitrary"),
        name="out_proj",
    )(ymix, ymem, w, x)


def _pool_kernel(u_ref, prev_ref, gate_ref, w_ref, scale_ref, o_ref, wb_ref, *, ts, tr):
    g = pl.program_id(0)
    si = pl.program_id(2)

    @pl.when(jnp.logical_and(pl.program_id(1) == 0, si == 0))
    def _():
        wb_ref[...] = w_ref[0].astype(BF16)

    for idx, window in enumerate(POOL_WINDOWS):
        @pl.when(g == idx)
        def _(window=window):
            _pool_tile(u_ref, prev_ref, gate_ref, scale_ref, o_ref, wb_ref, si,
                       ts=ts, tr=tr, window=window)


def _pool_tile(u_ref, prev_ref, gate_ref, scale_ref, o_ref, wb_ref, si, *, ts, tr, window):
    def pooled(rc):
        r0 = rc * tr
        if rc == 0:
            prev = jnp.where(si > 0, prev_ref[...].astype(F32), 0.0)
            xs = jnp.concatenate([prev, u_ref[0:tr, :].astype(F32)], axis=0)
        else:
            xs = u_ref[r0 - POOL_HALO:r0 + tr, :].astype(F32)
        s = xs
        k = 1
        while k < window:
            s = s + pltpu.roll(s, k, 0)
            k *= 2
        t = si * ts + r0 + lax.broadcasted_iota(jnp.int32, (tr, 1), 0)
        inv_cnt = 1.0 / jnp.minimum(t + 1, window).astype(F32)
        return (s[POOL_HALO:] * inv_cnt - xs[POOL_HALO:]).astype(BF16)

    n_sub = ts // tr
    nxt = pooled(0)
    for rc in range(n_sub):
        cur = nxt
        if rc + 1 < n_sub:
            nxt = pooled(rc + 1)
        rows = pl.ds(rc * tr, tr)
        mixed = jnp.dot(cur, wb_ref[...], preferred_element_type=F32)
        o_ref[rows, :] = (mixed * scale_ref[...] * gate_ref[rows, :].astype(F32)
                          ).astype(o_ref.dtype)


def _pool_mixer(u, gate, pool_w, pool_scale, *, batch, seq, ts=512, tr=256):
    t_total = u.shape[0]
    c = POOL_GROUP
    n_s = seq // ts
    halo_blocks_per_tile = ts // POOL_HALO

    def prev_map(g, b, si):
        first = (b * n_s + si) * halo_blocks_per_tile
        return (jnp.maximum(first - 1, 0), g)

    tile = pl.BlockSpec((ts, c), lambda g, b, si: (b * n_s + si, g))
    return pl.pallas_call(
        functools.partial(_pool_kernel, ts=ts, tr=tr),
        grid=(len(POOL_WINDOWS), batch, n_s),
        in_specs=[tile, pl.BlockSpec((POOL_HALO, c), prev_map), tile,
                  pl.BlockSpec((1, c, c), lambda g, b, si: (g, 0, 0)),
                  pl.BlockSpec((1, c), lambda g, b, si: (0, g))],
        out_specs=tile,
        out_shape=jax.ShapeDtypeStruct((t_total, D_MIX), BF16),
        scratch_shapes=[pltpu.VMEM((c, c), BF16)],
        compiler_params=_params("arbitrary", "arbitrary", "arbitrary"),
        name="pool_mixer",
    )(u, u, gate, pool_w, pool_scale.reshape(1, D_MIX).astype(F32))


def _diff_attn_kernel(lq1_ref, lk1_ref, lq2_ref, lk2_ref, slope_ref, pos_ref,
                      q_ref, k_ref, v_ref, gate_ref, g_ref, o_ref, kaug_ref,
                      *, tq, tr, n_q, lam_init):
    qi = pl.program_id(2)

    f = lambda ref: ref[...].astype(F32)
    lam = (jnp.exp(jnp.sum(f(lq1_ref) * f(lk1_ref), keepdims=True))
           - jnp.exp(jnp.sum(f(lq2_ref) * f(lk2_ref), keepdims=True)) + lam_init)

    @pl.when(qi == 0)
    def _():
        for j in range(2):
            kaug_ref[j, :, 0:DIFF_QK] = k_ref[:, j * DIFF_QK:(j + 1) * DIFF_QK]
            kaug_ref[j, :, DIFF_QK:] = pos_ref[...]

    slope_cols = jnp.broadcast_to(slope_ref[0], (tr, DIFF_QK))

    def logits(j, r0, n_keys):
        sl = slice(j * DIFF_QK, (j + 1) * DIFF_QK)
        q_aug = jnp.concatenate([q_ref[pl.ds(r0, tr), sl], slope_cols], axis=-1)
        return lax.dot_general(q_aug, kaug_ref[j, 0:n_keys, :],
                               (((1,), (1,)), ((), ())), preferred_element_type=F32)

    def softmax(s, n_keys):
        n_past = n_keys - tr
        causal = (lax.broadcasted_iota(jnp.int32, (1, tr), 1)
                  <= lax.broadcasted_iota(jnp.int32, (tr, 1), 0))
        own = jnp.where(causal, s[:, n_past:], MASK_VALUE)
        m = jnp.max(own, axis=-1, keepdims=True)
        if n_past:
            past = s[:, :n_past]
            m = jnp.maximum(m, jnp.max(past, axis=-1, keepdims=True))
        p = jnp.exp(own - m)
        if n_past:
            p = jnp.concatenate([jnp.exp(past - m), p], axis=-1)
        return p.astype(BF16), jnp.sum(p, axis=-1, keepdims=True)

    def finish(r0, outs):
        o = outs[0] - lam * outs[1]
        ms = jnp.mean(o * o, axis=-1, keepdims=True)
        y = (o * lax.rsqrt(ms + SUBLN_EPS) * g_ref[...]) * (1.0 - lam_init)
        gate = gate_ref[pl.ds(r0, tr), :].astype(F32)
        o_ref[pl.ds(r0, tr), :] = (y * gate).astype(o_ref.dtype)

    for c in range(n_q):
        @pl.when(qi == c)
        def _(c=c):
            chains = [(j, r * tr, c * tq + (r + 1) * tr)
                      for r in range(tq // tr) for j in range(2)]
            s_next = logits(*chains[0])
            outs = []
            for n, (j, r0, n_keys) in enumerate(chains):
                s = s_next
                if n + 1 < len(chains):
                    s_next = logits(*chains[n + 1])
                p, l = softmax(s, n_keys)
                outs.append(jnp.dot(p, v_ref[0:n_keys, :], preferred_element_type=F32) / l)
                if j == 1:
                    finish(r0, outs)
                    outs = []


def _diff_attention(qkv, gate, lam_vecs, subln_g, *, batch, seq, lam_init, tq=2048, tr=256):
    t_total = qkv.shape[0]
    n_q = seq // tq
    slope_cols, pos_cols = _alibi_tables(seq)
    vec_spec = pl.BlockSpec((1, DIFF_QK), lambda b, h, qi: (0, 0))
    lam_vecs = [v.reshape(1, DIFF_QK).astype(F32) for v in lam_vecs]
    return pl.pallas_call(
        functools.partial(_diff_attn_kernel, tq=tq, tr=tr, n_q=n_q, lam_init=lam_init),
        grid=(batch, DIFF_HEADS, n_q),
        in_specs=[
            vec_spec, vec_spec, vec_spec, vec_spec,
            pl.BlockSpec((1, 1, DIFF_QK), lambda b, h, qi: (h, 0, 0)),
            pl.BlockSpec((seq, DIFF_QK), lambda b, h, qi: (0, 0)),
            pl.BlockSpec((tq, DIFF_V), lambda b, h, qi: (b * n_q + qi, h)),
            pl.BlockSpec((seq, DIFF_V), lambda b, h, qi: (b, DIFF_HEADS + h)),
            pl.BlockSpec((seq, DIFF_V), lambda b, h, qi: (b, 2 * DIFF_HEADS + h)),
            pl.BlockSpec((tq, DIFF_V), lambda b, h, qi: (b * n_q + qi, h)),
            pl.BlockSpec((1, DIFF_V), lambda b, h, qi: (0, 0)),
        ],
        out_specs=pl.BlockSpec((tq, DIFF_V), lambda b, h, qi: (b * n_q + qi, h)),
        out_shape=jax.ShapeDtypeStruct((t_total, D_MIX), BF16),
        scratch_shapes=[pltpu.VMEM((2, seq, 2 * DIFF_QK), BF16)],
        compiler_params=_params("parallel", "parallel", "arbitrary"),
        name="diff_attention",
    )(*lam_vecs, slope_cols, pos_cols, qkv, qkv, qkv, gate,
      subln_g.reshape(1, DIFF_V).astype(F32))


def _mem_attn_kernel(q_ref, k_ref, v_ref, gate_ref, o_ref):
    scale = MEM_HEAD_DIM ** -0.5
    for h in range(MEM_HEADS):
        sl = slice(h * MEM_HEAD_DIM, (h + 1) * MEM_HEAD_DIM)
        s = lax.dot_general(q_ref[:, sl], k_ref[:, sl], (((1,), (1,)), ((), ())),
                            preferred_element_type=F32) * scale
        m = jnp.max(s, axis=-1, keepdims=True)
        p = jnp.exp(s - m)
        l = jnp.sum(p, axis=-1, keepdims=True)
        o = jnp.dot(p.astype(BF16), v_ref[:, sl], preferred_element_type=F32) / l
        o_ref[:, sl] = (o * gate_ref[:, sl].astype(F32)).astype(o_ref.dtype)


def _mem_attention(qm, kv, gate, *, batch, seq, mem_len, tq=512):
    t_total = qm.shape[0]
    n_q = seq // tq
    z_block = D_MIX // D_MEM
    return pl.pallas_call(
        _mem_attn_kernel,
        grid=(batch, n_q),
        in_specs=[
            pl.BlockSpec((tq, D_MEM), lambda b, qi: (b * n_q + qi, 0)),
            pl.BlockSpec((mem_len, D_MEM), lambda b, qi: (b, 0)),
            pl.BlockSpec((mem_len, D_MEM), lambda b, qi: (b, 1)),
            pl.BlockSpec((tq, D_MEM), lambda b, qi: (b * n_q + qi, z_block)),
        ],
        out_specs=pl.BlockSpec((tq, D_MEM), lambda b, qi: (b * n_q + qi, 0)),
        out_shape=jax.ShapeDtypeStruct((t_total, D_MEM), BF16),
        compiler_params=_params("parallel", "arbitrary"),
        name="mem_attention",
    )(qm, kv, kv, gate)


def _lambda_init(layer_idx):
    return 0.8 - 0.6 * math.exp(-0.3 * layer_idx)


def kernel(x, mem, l0_norm_g, l0_w_in, l0_pool_w, l0_pool_scale, l0_mem_norm_g,
           l0_w_mem_kv, l0_w_out, l1_norm_g, l1_w_in, l1_lambda_q1, l1_lambda_k1,
           l1_lambda_q2, l1_lambda_k2, l1_subln_g, l1_mem_norm_g, l1_w_mem_kv,
           l1_w_out, final_norm_g):
    batch, seq, d = x.shape
    mem_len = mem.shape[1]
    x2 = x.reshape(batch * seq, d)
    mem2 = mem.reshape(batch * mem_len, d)
    dims = dict(batch=batch, seq=seq)

    h = _rmsnorm(x2, l0_norm_g, BF16)
    u = _matmul(h, l0_w_in, 0, D_MIX)
    qm = _matmul(h, l0_w_in, D_MIX, D_MEM)
    gate = _matmul(h, l0_w_in, D_MIX + D_MEM, D_INNER, gate=True)
    kv = _matmul(_rmsnorm(mem2, l0_mem_norm_g, BF16), l0_w_mem_kv, 0, 2 * D_MEM)
    y_mix = _pool_mixer(u, gate, l0_pool_w, l0_pool_scale, **dims)
    y_mem = _mem_attention(qm, kv, gate, mem_len=mem_len, **dims)
    x2 = _out_proj(y_mix, y_mem, l0_w_out.astype(BF16), x2)

    h = _rmsnorm(x2, l1_norm_g, BF16)
    qkv = _matmul(h, l1_w_in, 0, 3 * D_MIX, scaled_cols=D_MIX, col_scale=DIFF_QK ** -0.5)
    qm = _matmul(h, l1_w_in, 3 * D_MIX, D_MEM)
    gate = _matmul(h, l1_w_in, 3 * D_MIX + D_MEM, D_INNER, gate=True)
    kv = _matmul(_rmsnorm(mem2, l1_mem_norm_g, BF16), l1_w_mem_kv, 0, 2 * D_MEM)
    y_mix = _diff_attention(qkv, gate, (l1_lambda_q1, l1_lambda_k1, l1_lambda_q2, l1_lambda_k2),
                            l1_subln_g, lam_init=_lambda_init(1), **dims)
    y_mem = _mem_attention(qm, kv, gate, mem_len=mem_len, **dims)
    x2 = _out_proj(y_mix, y_mem, l1_w_out.astype(BF16), x2)

    return _rmsnorm(x2, final_norm_g, F32).reshape(batch, seq, d)
```

```python
import functools
import math

import numpy as np
import jax
import jax.numpy as jnp
from jax import lax
from jax.experimental import pallas as pl
from jax.experimental.pallas import tpu as pltpu

F32 = jnp.float32
BF16 = jnp.bfloat16

D_MODEL = 4096
D_INNER = 2 * D_MODEL
D_MEM = D_INNER // 4
D_MIX = D_INNER - D_MEM
MEM_HEADS = 4
MEM_HEAD_DIM = D_MEM // MEM_HEADS
POOL_WINDOWS = (2, 4, 8, 16)
POOL_GROUP = D_MIX // len(POOL_WINDOWS)
POOL_HALO = 16
DIFF_QK = 128
DIFF_V = 2 * DIFF_QK
DIFF_HEADS = D_MIX // DIFF_V
RMS_EPS = 1e-6
SUBLN_EPS = 1e-5
MASK_VALUE = -1e30

V7X_VMEM_BYTES = 64 * 1024 * 1024
VMEM_LIMIT_BYTES = V7X_VMEM_BYTES - 8 * 1024 * 1024


def _params(*semantics):
    return pltpu.CompilerParams(dimension_semantics=semantics,
                                vmem_limit_bytes=VMEM_LIMIT_BYTES)


def _silu(z):
    half = 0.5 * z
    return half + half * jnp.tanh(half)


def _alibi_slopes(n):
    def pow2(m):
        start = 2.0 ** (-8.0 / m)
        return [start ** (i + 1) for i in range(m)]
    if math.log2(n).is_integer():
        s = pow2(n)
    else:
        c = 2 ** math.floor(math.log2(n))
        s = pow2(c) + pow2(2 * c)[0::2][: n - c]
    return np.asarray(s, dtype=np.float32)


POS_RADIX = 64
SLOPE_PIECES = 3


def _alibi_tables(seq):
    rest = _alibi_slopes(DIFF_HEADS).astype(np.float64)
    pieces = []
    for _ in range(SLOPE_PIECES):
        piece = rest.astype(BF16).astype(np.float64)
        pieces.append(piece)
        rest = rest - piece
    assert not rest.any(), "slope pieces must be exact"
    n = SLOPE_PIECES
    slope_cols = np.zeros((DIFF_HEADS, 1, DIFF_QK), np.float64)
    slope_cols[:, 0, 0:n] = POS_RADIX * np.stack(pieces, axis=-1)
    slope_cols[:, 0, n:2 * n] = np.stack(pieces, axis=-1)
    pos = np.arange(seq)
    pos_cols = np.zeros((seq, DIFF_QK), np.float64)
    pos_cols[:, 0:n] = (pos // POS_RADIX)[:, None]
    pos_cols[:, n:2 * n] = (pos % POS_RADIX)[:, None]
    return jnp.asarray(slope_cols, BF16), jnp.asarray(pos_cols, BF16)


def _rmsnorm_kernel(x_ref, g_ref, o_ref, *, eps):
    x = x_ref[...].astype(F32)
    ms = jnp.mean(x * x, axis=-1, keepdims=True)
    o_ref[...] = (x * lax.rsqrt(ms + eps) * g_ref[...]).astype(o_ref.dtype)


def _rmsnorm(x, g, out_dtype, *, eps=RMS_EPS, tm=512):
    m, d = x.shape
    return pl.pallas_call(
        functools.partial(_rmsnorm_kernel, eps=eps),
        grid=(m // tm,),
        in_specs=[pl.BlockSpec((tm, d), lambda i: (i, 0)),
                  pl.BlockSpec((1, d), lambda i: (0, 0))],
        out_specs=pl.BlockSpec((tm, d), lambda i: (i, 0)),
        out_shape=jax.ShapeDtypeStruct((m, d), out_dtype),
        compiler_params=_params("parallel"),
        name="rmsnorm",
    )(x, g.reshape(1, d).astype(F32))


def _matmul_kernel(*refs, gate, col0, bn, n_panels, scaled_cols, col_scale, cast_rows):
    if cast_rows:
        (a_ref, w_hbm, src_hbm, o_ref, dst_hbm, stage_ref, wb_ref, sem,
         cast_in, cast_out, cast_sems) = refs
    else:
        a_ref, w_hbm, o_ref, stage_ref, wb_ref, sem = refs
    j = pl.program_id(0)

    if cast_rows:
        step = j * pl.num_programs(1) + pl.program_id(1)
        n_steps = n_panels * pl.num_programs(1)

        def fetch(t):
            return pltpu.make_async_copy(src_hbm.at[pl.ds(t * cast_rows, cast_rows), :],
                                         cast_in, cast_sems.at[0])

        def put(t):
            return pltpu.make_async_copy(cast_out,
                                         dst_hbm.at[pl.ds(t * cast_rows, cast_rows), :],
                                         cast_sems.at[1])

        @pl.when(step == 0)
        def _():
            fetch(0).start()

        fetch(step).wait()

        @pl.when(step > 0)
        def _():
            put(step - 1).wait()

        cast_out[...] = cast_in[...].astype(BF16)
        put(step).start()

        @pl.when(step + 1 < n_steps)
        def _():
            fetch(step + 1).start()

    def panel_copy(jj):
        return pltpu.make_async_copy(w_hbm.at[:, pl.ds(col0 + jj * bn, bn)], stage_ref, sem)

    @pl.when(pl.program_id(1) == 0)
    def _():
        @pl.when(j == 0)
        def _():
            panel_copy(0).start()

        panel_copy(j).wait()
        wb_ref[...] = stage_ref[...].astype(BF16)

        @pl.when(j + 1 < n_panels)
        def _():
            panel_copy(j + 1).start()

    acc = jnp.dot(a_ref[...], wb_ref[...], preferred_element_type=F32)
    if gate:
        acc = _silu(acc)
    if scaled_cols:
        acc = acc * jnp.where(j < scaled_cols // bn, col_scale, 1.0)
    o_ref[...] = acc.astype(o_ref.dtype)

    if cast_rows:
        @pl.when(step == n_steps - 1)
        def _():
            put(step).wait()


def _matmul(a, w, col0, ncols, out_dtype=BF16, *, gate=False, scaled_cols=0, col_scale=1.0,
            also_cast=None, bm=1024, bn=1024):
    m, k = a.shape
    bm = min(bm, m)
    assert ncols % bn == 0 and m % bm == 0 and scaled_cols % bn == 0
    n_panels = ncols // bn
    grid = (n_panels, m // bm)
    in_specs = [pl.BlockSpec((bm, k), lambda j, i: (i, 0)), pl.BlockSpec(memory_space=pl.ANY)]
    out_specs = [pl.BlockSpec((bm, bn), lambda j, i: (i, j))]
    out_shape = [jax.ShapeDtypeStruct((m, ncols), out_dtype)]
    scratch = [pltpu.VMEM((k, bn), F32), pltpu.VMEM((k, bn), BF16), pltpu.SemaphoreType.DMA(())]
    args = [a, w]
    cast_rows = 0
    if also_cast is not None:
        rows, cols = also_cast.shape
        cast_rows = rows // (grid[0] * grid[1])
        assert cast_rows * grid[0] * grid[1] == rows
        in_specs.append(pl.BlockSpec(memory_space=pl.ANY))
        out_specs.append(pl.BlockSpec(memory_space=pl.ANY))
        out_shape.append(jax.ShapeDtypeStruct((rows, cols), BF16))
        scratch += [pltpu.VMEM((cast_rows, cols), F32), pltpu.VMEM((cast_rows, cols), BF16),
                    pltpu.SemaphoreType.DMA((2,))]
        args.append(also_cast)
    out = pl.pallas_call(
        functools.partial(_matmul_kernel, gate=gate, col0=col0, bn=bn, n_panels=n_panels,
                          scaled_cols=scaled_cols, col_scale=col_scale, cast_rows=cast_rows),
        grid=grid,
        in_specs=in_specs,
        out_specs=out_specs,
        out_shape=out_shape,
        scratch_shapes=scratch,
        compiler_params=_params("arbitrary", "arbitrary"),
        name="matmul",
    )(*args)
    return out if also_cast is not None else out[0]


OUT_BK = D_MEM
OUT_MIX_CHUNKS = D_MIX // OUT_BK


def _out_proj_kernel(ymix_ref, ymem_ref, w_ref, x_ref, o_ref):
    kk = pl.program_id(2)

    @pl.when(kk == 0)
    def _():
        o_ref[...] = x_ref[...] + jnp.dot(ymix_ref[...], w_ref[...],
                                          preferred_element_type=F32)

    @pl.when(jnp.logical_and(kk > 0, kk < OUT_MIX_CHUNKS))
    def _():
        o_ref[...] += jnp.dot(ymix_ref[...], w_ref[...],
                              preferred_element_type=F32)

    @pl.when(kk == OUT_MIX_CHUNKS)
    def _():
        o_ref[...] += jnp.dot(ymem_ref[...], w_ref[...],
                              preferred_element_type=F32)


def _out_proj(ymix, ymem, w, x, *, bm=1024, bn=1024):
    m = x.shape[0]
    n = w.shape[1]
    last = OUT_MIX_CHUNKS - 1
    return pl.pallas_call(
        _out_proj_kernel,
        grid=(m // bm, n // bn, OUT_MIX_CHUNKS + 1),
        in_specs=[
            pl.BlockSpec((bm, OUT_BK), lambda i, j, k: (i, jnp.minimum(k, last))),
            pl.BlockSpec((bm, OUT_BK), lambda i, j, k: (i, 0)),
            pl.BlockSpec((OUT_BK, bn), lambda i, j, k: (k, j)),
            pl.BlockSpec((bm, bn), lambda i, j, k: (i, j)),
        ],
        out_specs=pl.BlockSpec((bm, bn), lambda i, j, k: (i, j)),
        out_shape=jax.ShapeDtypeStruct((m, n), F32),
        compiler_params=_params("parallel", "parallel", "arbitrary"),
        name="out_proj",
    )(ymix, ymem, w, x)


def _pool_kernel(u_ref, prev_ref, gate_ref, w_ref, scale_ref, o_ref, wb_ref, *, ts, tr):
    g = pl.program_id(0)
    si = pl.program_id(2)

    @pl.when(jnp.logical_and(pl.program_id(1) == 0, si == 0))
    def _():
        wb_ref[...] = w_ref[0].astype(BF16)

    for idx, window in enumerate(POOL_WINDOWS):
        @pl.when(g == idx)
        def _(window=window):
            _pool_tile(u_ref, prev_ref, gate_ref, scale_ref, o_ref, wb_ref, si,
                       ts=ts, tr=tr, window=window)


def _pool_tile(u_ref, prev_ref, gate_ref, scale_ref, o_ref, wb_ref, si, *, ts, tr, window):
    def pooled(rc):
        r0 = rc * tr
        if rc == 0:
            prev = jnp.where(si > 0, prev_ref[...].astype(F32), 0.0)
            xs = jnp.concatenate([prev, u_ref[0:tr, :].astype(F32)], axis=0)
        else:
            xs = u_ref[r0 - POOL_HALO:r0 + tr, :].astype(F32)
        s = xs
        k = 1
        while k < window:
            s = s + pltpu.roll(s, k, 0)
            k *= 2
        t = si * ts + r0 + lax.broadcasted_iota(jnp.int32, (tr, 1), 0)
        inv_cnt = 1.0 / jnp.minimum(t + 1, window).astype(F32)
        return (s[POOL_HALO:] * inv_cnt - xs[POOL_HALO:]).astype(BF16)

    n_sub = ts // tr
    nxt = pooled(0)
    for rc in range(n_sub):
        cur = nxt
        if rc + 1 < n_sub:
            nxt = pooled(rc + 1)
        rows = pl.ds(rc * tr, tr)
        mixed = jnp.dot(cur, wb_ref[...], preferred_element_type=F32)
        o_ref[rows, :] = (mixed * scale_ref[...] * gate_ref[rows, :].astype(F32)
                          ).astype(o_ref.dtype)


def _pool_mixer(u, gate, pool_w, pool_scale, *, batch, seq, ts=512, tr=256):
    t_total = u.shape[0]
    c = POOL_GROUP
    n_s = seq // ts
    halo_blocks_per_tile = ts // POOL_HALO

    def prev_map(g, b, si):
        first = (b * n_s + si) * halo_blocks_per_tile
        return (jnp.maximum(first - 1, 0), g)

    tile = pl.BlockSpec((ts, c), lambda g, b, si: (b * n_s + si, g))
    return pl.pallas_call(
        functools.partial(_pool_kernel, ts=ts, tr=tr),
        grid=(len(POOL_WINDOWS), batch, n_s),
        in_specs=[tile, pl.BlockSpec((POOL_HALO, c), prev_map), tile,
                  pl.BlockSpec((1, c, c), lambda g, b, si: (g, 0, 0)),
                  pl.BlockSpec((1, c), lambda g, b, si: (0, g))],
        out_specs=tile,
        out_shape=jax.ShapeDtypeStruct((t_total, D_MIX), BF16),
        scratch_shapes=[pltpu.VMEM((c, c), BF16)],
        compiler_params=_params("arbitrary", "arbitrary", "arbitrary"),
        name="pool_mixer",
    )(u, u, gate, pool_w, pool_scale.reshape(1, D_MIX).astype(F32))


def _diff_attn_kernel(lq1_ref, lk1_ref, lq2_ref, lk2_ref, slope_ref, pos_ref,
                      q_ref, k_ref, v_ref, gate_ref, g_ref, o_ref, kaug_ref,
                      *, tq, tr, n_q, lam_init):
    qi = pl.program_id(2)

    f = lambda ref: ref[...].astype(F32)
    lam = (jnp.exp(jnp.sum(f(lq1_ref) * f(lk1_ref), keepdims=True))
           - jnp.exp(jnp.sum(f(lq2_ref) * f(lk2_ref), keepdims=True)) + lam_init)

    @pl.when(qi == 0)
    def _():
        for j in range(2):
            kaug_ref[j, :, 0:DIFF_QK] = k_ref[:, j * DIFF_QK:(j + 1) * DIFF_QK]
            kaug_ref[j, :, DIFF_QK:] = pos_ref[...]

    slope_cols = jnp.broadcast_to(slope_ref[0], (tr, DIFF_QK))

    def logits(j, r0, n_keys):
        sl = slice(j * DIFF_QK, (j + 1) * DIFF_QK)
        q_aug = jnp.concatenate([q_ref[pl.ds(r0, tr), sl], slope_cols], axis=-1)
        return lax.dot_general(q_aug, kaug_ref[j, 0:n_keys, :],
                               (((1,), (1,)), ((), ())), preferred_element_type=F32)

    def softmax(s, n_keys):
        n_past = n_keys - tr
        causal = (lax.broadcasted_iota(jnp.int32, (1, tr), 1)
                  <= lax.broadcasted_iota(jnp.int32, (tr, 1), 0))
        own = jnp.where(causal, s[:, n_past:], MASK_VALUE)
        m = jnp.max(own, axis=-1, keepdims=True)
        if n_past:
            past = s[:, :n_past]
            m = jnp.maximum(m, jnp.max(past, axis=-1, keepdims=True))
        p = jnp.exp(own - m)
        if n_past:
            p = jnp.concatenate([jnp.exp(past - m), p], axis=-1)
        return p.astype(BF16), jnp.sum(p, axis=-1, keepdims=True)

    def finish(r0, outs):
        o = outs[0] - lam * outs[1]
        ms = jnp.mean(o * o, axis=-1, keepdims=True)
        y = (o * lax.rsqrt(ms + SUBLN_EPS) * g_ref[...]) * (1.0 - lam_init)
        gate = gate_ref[pl.ds(r0, tr), :].astype(F32)
        o_ref[pl.ds(r0, tr), :] = (y * gate).astype(o_ref.dtype)

    for c in range(n_q):
        @pl.when(qi == c)
        def _(c=c):
            chains = [(j, r * tr, c * tq + (r + 1) * tr)
                      for r in range(tq // tr) for j in range(2)]
            chains = chains[::-1]
            s_next = logits(*chains[0])
            outs = {}
            for n, (j, r0, n_keys) in enumerate(chains):
                s = s_next
                if n + 1 < len(chains):
                    s_next = logits(*chains[n + 1])
                p, l = softmax(s, n_keys)
                outs[j] = jnp.dot(p, v_ref[0:n_keys, :], preferred_element_type=F32) / l
                if len(outs) == 2:
                    finish(r0, outs)
                    outs = {}


def _diff_attention(qkv, gate, lam_vecs, subln_g, *, batch, seq, lam_init, tq=2048, tr=256):
    t_total = qkv.shape[0]
    n_q = seq // tq
    slope_cols, pos_cols = _alibi_tables(seq)
    vec_spec = pl.BlockSpec((1, DIFF_QK), lambda b, h, qi: (0, 0))
    lam_vecs = [v.reshape(1, DIFF_QK).astype(F32) for v in lam_vecs]
    return pl.pallas_call(
        functools.partial(_diff_attn_kernel, tq=tq, tr=tr, n_q=n_q, lam_init=lam_init),
        grid=(batch, DIFF_HEADS, n_q),
        in_specs=[
            vec_spec, vec_spec, vec_spec, vec_spec,
            pl.BlockSpec((1, 1, DIFF_QK), lambda b, h, qi: (h, 0, 0)),
            pl.BlockSpec((seq, DIFF_QK), lambda b, h, qi: (0, 0)),
            pl.BlockSpec((tq, DIFF_V), lambda b, h, qi: (b * n_q + qi, h)),
            pl.BlockSpec((seq, DIFF_V), lambda b, h, qi: (b, DIFF_HEADS + h)),
            pl.BlockSpec((seq, DIFF_V), lambda b, h, qi: (b, 2 * DIFF_HEADS + h)),
            pl.BlockSpec((tq, DIFF_V), lambda b, h, qi: (b * n_q + qi, h)),
            pl.BlockSpec((1, DIFF_V), lambda b, h, qi: (0, 0)),
        ],
        out_specs=pl.BlockSpec((tq, DIFF_V), lambda b, h, qi: (b * n_q + qi, h)),
        out_shape=jax.ShapeDtypeStruct((t_total, D_MIX), BF16),
        scratch_shapes=[pltpu.VMEM((2, seq, 2 * DIFF_QK), BF16)],
        compiler_params=_params("parallel", "parallel", "arbitrary"),
        name="diff_attention",
    )(*lam_vecs, slope_cols, pos_cols, qkv, qkv, qkv, gate,
      subln_g.reshape(1, DIFF_V).astype(F32))


def _mem_attn_kernel(q_ref, k_ref, v_ref, gate_ref, o_ref):
    scale = MEM_HEAD_DIM ** -0.5
    for h in range(MEM_HEADS):
        sl = slice(h * MEM_HEAD_DIM, (h + 1) * MEM_HEAD_DIM)
        s = lax.dot_general(q_ref[:, sl], k_ref[:, sl], (((1,), (1,)), ((), ())),
                            preferred_element_type=F32) * scale
        m = jnp.max(s, axis=-1, keepdims=True)
        p = jnp.exp(s - m)
        l = jnp.sum(p, axis=-1, keepdims=True)
        o = jnp.dot(p.astype(BF16), v_ref[:, sl], preferred_element_type=F32) / l
        o_ref[:, sl] = (o * gate_ref[:, sl].astype(F32)).astype(o_ref.dtype)


def _mem_attention(qm, kv, gate, *, batch, seq, mem_len, tq=512):
    t_total = qm.shape[0]
    n_q = seq // tq
    z_block = D_MIX // D_MEM
    return pl.pallas_call(
        _mem_attn_kernel,
        grid=(batch, n_q),
        in_specs=[
            pl.BlockSpec((tq, D_MEM), lambda b, qi: (b * n_q + qi, 0)),
            pl.BlockSpec((mem_len, D_MEM), lambda b, qi: (b, 0)),
            pl.BlockSpec((mem_len, D_MEM), lambda b, qi: (b, 1)),
            pl.BlockSpec((tq, D_MEM), lambda b, qi: (b * n_q + qi, z_block)),
        ],
        out_specs=pl.BlockSpec((tq, D_MEM), lambda b, qi: (b * n_q + qi, 0)),
        out_shape=jax.ShapeDtypeStruct((t_total, D_MEM), BF16),
        compiler_params=_params("parallel", "arbitrary"),
        name="mem_attention",
    )(qm, kv, kv, gate)


KV_BN = 512


def _lambda_init(layer_idx):
    return 0.8 - 0.6 * math.exp(-0.3 * layer_idx)


def kernel(x, mem, l0_norm_g, l0_w_in, l0_pool_w, l0_pool_scale, l0_mem_norm_g,
           l0_w_mem_kv, l0_w_out, l1_norm_g, l1_w_in, l1_lambda_q1, l1_lambda_k1,
           l1_lambda_q2, l1_lambda_k2, l1_subln_g, l1_mem_norm_g, l1_w_mem_kv,
           l1_w_out, final_norm_g):
    batch, seq, d = x.shape
    mem_len = mem.shape[1]
    x2 = x.reshape(batch * seq, d)
    mem2 = mem.reshape(batch * mem_len, d)
    dims = dict(batch=batch, seq=seq)

    h = _rmsnorm(x2, l0_norm_g, BF16)
    u = _matmul(h, l0_w_in, 0, D_MIX)
    qm = _matmul(h, l0_w_in, D_MIX, D_MEM)
    gate, w_out = _matmul(h, l0_w_in, D_MIX + D_MEM, D_INNER, gate=True, also_cast=l0_w_out)
    kv = _matmul(_rmsnorm(mem2, l0_mem_norm_g, BF16), l0_w_mem_kv, 0, 2 * D_MEM, bn=KV_BN)
    y_mix = _pool_mixer(u, gate, l0_pool_w, l0_pool_scale, **dims)
    y_mem = _mem_attention(qm, kv, gate, mem_len=mem_len, **dims)
    x2 = _out_proj(y_mix, y_mem, w_out, x2)

    h = _rmsnorm(x2, l1_norm_g, BF16)
    qkv = _matmul(h, l1_w_in, 0, 3 * D_MIX, scaled_cols=D_MIX, col_scale=DIFF_QK ** -0.5)
    qm = _matmul(h, l1_w_in, 3 * D_MIX, D_MEM)
    gate, w_out = _matmul(h, l1_w_in, 3 * D_MIX + D_MEM, D_INNER, gate=True,
                          also_cast=l1_w_out)
    kv = _matmul(_rmsnorm(mem2, l1_mem_norm_g, BF16), l1_w_mem_kv, 0, 2 * D_MEM, bn=KV_BN)
    y_mix = _diff_attention(qkv, gate, (l1_lambda_q1, l1_lambda_k1, l1_lambda_q2, l1_lambda_k2),
                            l1_subln_g, lam_init=_lambda_init(1), **dims)
    y_mem = _mem_attention(qm, kv, gate, mem_len=mem_len, **dims)
    x2 = _out_proj(y_mix, y_mem, w_out, x2)

    return _rmsnorm(x2, final_norm_g, F32).reshape(batch, seq, d)
```

```python
import functools
import math

import numpy as np
import jax
import jax.numpy as jnp
from jax import lax
from jax.experimental import pallas as pl
from jax.experimental.pallas import tpu as pltpu

F32 = jnp.float32
BF16 = jnp.bfloat16

D_MODEL = 4096
D_INNER = 2 * D_MODEL
D_MEM = D_INNER // 4
D_MIX = D_INNER - D_MEM
MEM_HEADS = 4
MEM_HEAD_DIM = D_MEM // MEM_HEADS
POOL_WINDOWS = (2, 4, 8, 16)
POOL_GROUP = D_MIX // len(POOL_WINDOWS)
POOL_HALO = 16
DIFF_QK = 128
DIFF_V = 2 * DIFF_QK
DIFF_HEADS = D_MIX // DIFF_V
RMS_EPS = 1e-6
SUBLN_EPS = 1e-5
MASK_VALUE = -1e30
LANES = 128
CAST_ROWS = 128

V7X_VMEM_BYTES = 64 * 1024 * 1024
VMEM_LIMIT_BYTES = V7X_VMEM_BYTES - 4 * 1024 * 1024


def _params(*semantics):
    return pltpu.CompilerParams(dimension_semantics=semantics,
                                vmem_limit_bytes=VMEM_LIMIT_BYTES)


def _silu(z):
    half = 0.5 * z
    return half + half * jnp.tanh(half)


def _alibi_slopes(n):
    def pow2(m):
        start = 2.0 ** (-8.0 / m)
        return [start ** (i + 1) for i in range(m)]
    if math.log2(n).is_integer():
        s = pow2(n)
    else:
        c = 2 ** math.floor(math.log2(n))
        s = pow2(c) + pow2(2 * c)[0::2][: n - c]
    return np.asarray(s, dtype=np.float32)


POS_RADIX = 64
SLOPE_PIECES = 3


def _alibi_tables(seq):
    rest = _alibi_slopes(DIFF_HEADS).astype(np.float64)
    pieces = []
    for _ in range(SLOPE_PIECES):
        piece = rest.astype(BF16).astype(np.float64)
        pieces.append(piece)
        rest = rest - piece
    assert not rest.any(), "slope pieces must be exact"
    n = SLOPE_PIECES
    slope_cols = np.zeros((DIFF_HEADS, 1, DIFF_QK), np.float64)
    slope_cols[:, 0, 0:n] = POS_RADIX * np.stack(pieces, axis=-1)
    slope_cols[:, 0, n:2 * n] = np.stack(pieces, axis=-1)
    pos = np.arange(seq)
    pos_cols = np.zeros((seq, DIFF_QK), np.float64)
    pos_cols[:, 0:n] = (pos // POS_RADIX)[:, None]
    pos_cols[:, n:2 * n] = (pos % POS_RADIX)[:, None]
    return jnp.asarray(slope_cols, BF16), jnp.asarray(pos_cols, BF16)


def _rmsnorm_kernel(x_ref, g_ref, o_ref, *, eps):
    x = x_ref[...].astype(F32)
    ms = jnp.mean(x * x, axis=-1, keepdims=True)
    o_ref[...] = (x * lax.rsqrt(ms + eps) * g_ref[...]).astype(o_ref.dtype)


def _rmsnorm(x, g, out_dtype, *, eps=RMS_EPS, tm=512):
    m, d = x.shape
    return pl.pallas_call(
        functools.partial(_rmsnorm_kernel, eps=eps),
        grid=(m // tm,),
        in_specs=[pl.BlockSpec((tm, d), lambda i: (i, 0)),
                  pl.BlockSpec((1, d), lambda i: (0, 0))],
        out_specs=pl.BlockSpec((tm, d), lambda i: (i, 0)),
        out_shape=jax.ShapeDtypeStruct((m, d), out_dtype),
        compiler_params=_params("parallel"),
        name="rmsnorm",
    )(x, g.reshape(1, d).astype(F32))


PANEL_PLAIN, PANEL_GATE, PANEL_SCALED = 0, 1, 2


def _proj_kernel(*refs, bn, n_panels, kinds, col_scale, has_ssq, n_cast, cast_rows, d_norm):
    refs = list(refs)
    col_ref, kind_ref, a_ref, w_hbm = refs[:4]
    del refs[:4]
    ssq_ref = refs.pop(0) if has_ssq else None
    src_hbm = refs.pop(0) if n_cast else None
    o_ref = refs.pop(0)
    dst_hbm = refs.pop(0) if n_cast else None
    stage_ref, wb_ref, sem = refs[:3]
    if n_cast:
        cast_in, cast_out, cast_sems = refs[3:]
    j = pl.program_id(0)

    if n_cast:
        step = j * pl.num_programs(1) + pl.program_id(1)

        def fetch(t):
            return pltpu.make_async_copy(src_hbm.at[pl.ds(t * cast_rows, cast_rows), :],
                                         cast_in, cast_sems.at[0])

        def put(t):
            return pltpu.make_async_copy(cast_out,
                                         dst_hbm.at[pl.ds(t * cast_rows, cast_rows), :],
                                         cast_sems.at[1])

        @pl.when(step == 0)
        def _():
            fetch(0).start()

        @pl.when(jnp.logical_and(step > 0, step <= n_cast))
        def _():
            put(step - 1).wait()

        @pl.when(step < n_cast)
        def _():
            fetch(step).wait()
            cast_out[...] = cast_in[...].astype(BF16)
            put(step).start()

            @pl.when(step + 1 < n_cast)
            def _():
                fetch(step + 1).start()

    def panel_copy(jj):
        col = pl.multiple_of(col_ref[jj], bn)
        return pltpu.make_async_copy(w_hbm.at[:, pl.ds(col, bn)], stage_ref, sem)

    @pl.when(pl.program_id(1) == 0)
    def _():
        @pl.when(j == 0)
        def _():
            panel_copy(0).start()

        panel_copy(j).wait()
        wb_ref[...] = stage_ref[...].astype(BF16)

        @pl.when(j + 1 < n_panels)
        def _():
            panel_copy(j + 1).start()

    acc = jnp.dot(a_ref[...], wb_ref[...], preferred_element_type=F32)
    if has_ssq:
        inv_rms = lax.rsqrt(ssq_ref[...] * (1.0 / d_norm) + RMS_EPS)
        acc = acc * jnp.concatenate([inv_rms] * (bn // LANES), axis=-1)
    kind = kind_ref[j]
    if PANEL_GATE in kinds:
        acc = jnp.where(kind == PANEL_GATE, _silu(acc), acc)
    if PANEL_SCALED in kinds:
        acc = acc * jnp.where(kind == PANEL_SCALED, col_scale, 1.0)
    o_ref[...] = acc.astype(o_ref.dtype)


def _projection(a, w, segments, *, col_scale=1.0, row_ssq=None, also_cast=None,
                bm=1024, bn=1024):
    m, k = a.shape
    bm = min(bm, m)
    cols, kinds = [], []
    for col0, width, kind in segments:
        assert col0 % bn == 0 and width % bn == 0
        cols += list(range(col0, col0 + width, bn))
        kinds += [kind] * (width // bn)
    n_panels = len(cols)
    grid = (n_panels, m // bm)
    n_steps = grid[0] * grid[1]
    smem = pl.BlockSpec(memory_space=pltpu.SMEM)
    in_specs = [smem, smem, pl.BlockSpec((bm, k), lambda j, i: (i, 0)),
                pl.BlockSpec(memory_space=pl.ANY)]
    args = [jnp.asarray(cols, jnp.int32), jnp.asarray(kinds, jnp.int32), a, w]
    out_specs = [pl.BlockSpec((bm, bn), lambda j, i: (i, j))]
    out_shape = [jax.ShapeDtypeStruct((m, n_panels * bn), BF16)]
    scratch = [pltpu.VMEM((k, bn), F32), pltpu.VMEM((k, bn), BF16), pltpu.SemaphoreType.DMA(())]
    if row_ssq is not None:
        in_specs.append(pl.BlockSpec((bm, row_ssq.shape[1]), lambda j, i: (i, 0)))
        args.append(row_ssq)
    n_cast = 0
    if also_cast is not None:
        rows, width = also_cast.shape
        n_cast = rows // CAST_ROWS
        assert n_cast * CAST_ROWS == rows and n_cast < n_steps
        in_specs.append(pl.BlockSpec(memory_space=pl.ANY))
        args.append(also_cast)
        out_specs.append(pl.BlockSpec(memory_space=pl.ANY))
        out_shape.append(jax.ShapeDtypeStruct((rows, width), BF16))
        scratch += [pltpu.VMEM((CAST_ROWS, width), F32), pltpu.VMEM((CAST_ROWS, width), BF16),
                    pltpu.SemaphoreType.DMA((2,))]
    out = pl.pallas_call(
        functools.partial(_proj_kernel, bn=bn, n_panels=n_panels, kinds=frozenset(kinds),
                          col_scale=col_scale, has_ssq=row_ssq is not None, n_cast=n_cast,
                          cast_rows=CAST_ROWS, d_norm=k),
        grid=grid,
        in_specs=in_specs,
        out_specs=out_specs,
        out_shape=out_shape,
        scratch_shapes=scratch,
        compiler_params=_params("arbitrary", "arbitrary"),
        name="projection",
    )(*args)
    return out if also_cast is not None else out[0]


OUT_BK = D_MEM
OUT_MIX_CHUNKS = D_MIX // OUT_BK


def _out_proj_kernel(ymix_ref, ymem_ref, w_ref, x_ref, *rest, with_norm):
    if with_norm:
        g_ref, o_ref, xg_ref, ssq_ref = rest
    else:
        (o_ref,) = rest
    kk = pl.program_id(2)

    @pl.when(kk == 0)
    def _():
        o_ref[...] = x_ref[...] + jnp.dot(ymix_ref[...], w_ref[...],
                                          preferred_element_type=F32)

    @pl.when(jnp.logical_and(kk > 0, kk < OUT_MIX_CHUNKS))
    def _():
        o_ref[...] += jnp.dot(ymix_ref[...], w_ref[...],
                              preferred_element_type=F32)

    @pl.when(kk == OUT_MIX_CHUNKS)
    def _():
        y = o_ref[...] + jnp.dot(ymem_ref[...], w_ref[...], preferred_element_type=F32)
        o_ref[...] = y
        if with_norm:
            xg_ref[...] = (y * g_ref[...]).astype(xg_ref.dtype)
            part = jnp.broadcast_to(jnp.sum(y * y, axis=-1, keepdims=True), ssq_ref.shape)

            @pl.when(pl.program_id(1) == 0)
            def _():
                ssq_ref[...] = part

            @pl.when(pl.program_id(1) > 0)
            def _():
                ssq_ref[...] += part


def _out_proj(ymix, ymem, w, x, norm_g=None, *, bm=1024, bn=1024):
    m = x.shape[0]
    n = w.shape[1]
    last = OUT_MIX_CHUNKS - 1
    tile = pl.BlockSpec((bm, bn), lambda i, j, k: (i, j))
    in_specs = [
        pl.BlockSpec((bm, OUT_BK), lambda i, j, k: (i, jnp.minimum(k, last))),
        pl.BlockSpec((bm, OUT_BK), lambda i, j, k: (i, 0)),
        pl.BlockSpec((OUT_BK, bn), lambda i, j, k: (k, j)),
        tile,
    ]
    args = [ymix, ymem, w, x]
    out_specs = [tile]
    out_shape = [jax.ShapeDtypeStruct((m, n), F32)]
    if norm_g is not None:
        in_specs.append(pl.BlockSpec((1, bn), lambda i, j, k: (0, j)))
        args.append(norm_g.reshape(1, n).astype(F32))
        out_specs += [tile, pl.BlockSpec((bm, LANES), lambda i, j, k: (i, 0))]
        out_shape += [jax.ShapeDtypeStruct((m, n), BF16),
                      jax.ShapeDtypeStruct((m, LANES), F32)]
    out = pl.pallas_call(
        functools.partial(_out_proj_kernel, with_norm=norm_g is not None),
        grid=(m // bm, n // bn, OUT_MIX_CHUNKS + 1),
        in_specs=in_specs,
        out_specs=out_specs,
        out_shape=out_shape,
        compiler_params=_params("parallel", "arbitrary", "arbitrary"),
        name="out_proj",
    )(*args)
    return out if norm_g is not None else out[0]


def _pool_kernel(u_ref, prev_ref, gate_ref, w_ref, scale_ref, o_ref, wb_ref, *, ts, tr):
    g = pl.program_id(0)
    si = pl.program_id(2)

    @pl.when(jnp.logical_and(pl.program_id(1) == 0, si == 0))
    def _():
        wb_ref[...] = w_ref[0].astype(BF16)

    for idx, window in enumerate(POOL_WINDOWS):
        @pl.when(g == idx)
        def _(window=window):
            _pool_tile(u_ref, prev_ref, gate_ref, scale_ref, o_ref, wb_ref, si,
                       ts=ts, tr=tr, window=window)


def _pool_tile(u_ref, prev_ref, gate_ref, scale_ref, o_ref, wb_ref, si, *, ts, tr, window):
    def pooled(rc):
        r0 = rc * tr
        if rc == 0:
            prev = jnp.where(si > 0, prev_ref[...].astype(F32), 0.0)
            xs = jnp.concatenate([prev, u_ref[0:tr, :].astype(F32)], axis=0)
        else:
            xs = u_ref[r0 - POOL_HALO:r0 + tr, :].astype(F32)
        s = xs
        k = 1
        while k < window:
            s = s + pltpu.roll(s, k, 0)
            k *= 2
        t = si * ts + r0 + lax.broadcasted_iota(jnp.int32, (tr, 1), 0)
        inv_cnt = 1.0 / jnp.minimum(t + 1, window).astype(F32)
        return (s[POOL_HALO:] * inv_cnt - xs[POOL_HALO:]).astype(BF16)

    n_sub = ts // tr
    nxt = pooled(0)
    for rc in range(n_sub):
        cur = nxt
        if rc + 1 < n_sub:
            nxt = pooled(rc + 1)
        rows = pl.ds(rc * tr, tr)
        mixed = jnp.dot(cur, wb_ref[...], preferred_element_type=F32)
        o_ref[rows, :] = (mixed * scale_ref[...] * gate_ref[rows, :].astype(F32)
                          ).astype(o_ref.dtype)


def _pool_mixer(proj, gate_col, pool_w, pool_scale, *, batch, seq, ts=512, tr=256):
    t_total = proj.shape[0]
    c = POOL_GROUP
    n_s = seq // ts
    halo_blocks_per_tile = ts // POOL_HALO
    assert gate_col % c == 0
    gate_block = gate_col // c

    def prev_map(g, b, si):
        first = (b * n_s + si) * halo_blocks_per_tile
        return (jnp.maximum(first - 1, 0), g)

    tile = pl.BlockSpec((ts, c), lambda g, b, si: (b * n_s + si, g))
    return pl.pallas_call(
        functools.partial(_pool_kernel, ts=ts, tr=tr),
        grid=(len(POOL_WINDOWS), batch, n_s),
        in_specs=[tile, pl.BlockSpec((POOL_HALO, c), prev_map),
                  pl.BlockSpec((ts, c), lambda g, b, si: (b * n_s + si, gate_block + g)),
                  pl.BlockSpec((1, c, c), lambda g, b, si: (g, 0, 0)),
                  pl.BlockSpec((1, c), lambda g, b, si: (0, g))],
        out_specs=tile,
        out_shape=jax.ShapeDtypeStruct((t_total, D_MIX), BF16),
        scratch_shapes=[pltpu.VMEM((c, c), BF16)],
        compiler_params=_params("arbitrary", "arbitrary", "arbitrary"),
        name="pool_mixer",
    )(proj, proj, proj, pool_w, pool_scale.reshape(1, D_MIX).astype(F32))


def _diff_attn_kernel(lq1_ref, lk1_ref, lq2_ref, lk2_ref, slope_ref, pos_ref,
                      q_ref, k_ref, v_ref, gate_ref, g_ref, o_ref, kaug_ref,
                      *, tq, tr, n_q, lam_init):
    qi = pl.program_id(2)

    f = lambda ref: ref[...].astype(F32)
    lam = (jnp.exp(jnp.sum(f(lq1_ref) * f(lk1_ref), keepdims=True))
           - jnp.exp(jnp.sum(f(lq2_ref) * f(lk2_ref), keepdims=True)) + lam_init)

    @pl.when(qi == 0)
    def _():
        for j in range(2):
            kaug_ref[j, :, 0:DIFF_QK] = k_ref[:, j * DIFF_QK:(j + 1) * DIFF_QK]
            kaug_ref[j, :, DIFF_QK:] = pos_ref[...]

    slope_cols = jnp.broadcast_to(slope_ref[0], (tr, DIFF_QK))

    def logits(j, r0, n_keys):
        sl = slice(j * DIFF_QK, (j + 1) * DIFF_QK)
        q_aug = jnp.concatenate([q_ref[pl.ds(r0, tr), sl], slope_cols], axis=-1)
        return lax.dot_general(q_aug, kaug_ref[j, 0:n_keys, :],
                               (((1,), (1,)), ((), ())), preferred_element_type=F32)

    def softmax(s, n_keys):
        n_past = n_keys - tr
        causal = (lax.broadcasted_iota(jnp.int32, (1, tr), 1)
                  <= lax.broadcasted_iota(jnp.int32, (tr, 1), 0))
        own = jnp.where(causal, s[:, n_past:], MASK_VALUE)
        m = jnp.max(own, axis=-1, keepdims=True)
        if n_past:
            past = s[:, :n_past]
            m = jnp.maximum(m, jnp.max(past, axis=-1, keepdims=True))
        p = jnp.exp(own - m)
        if n_past:
            p = jnp.concatenate([jnp.exp(past - m), p], axis=-1)
        return p.astype(BF16), jnp.sum(p, axis=-1, keepdims=True)

    def finish(r0, outs):
        o = outs[0] - lam * outs[1]
        ms = jnp.mean(o * o, axis=-1, keepdims=True)
        y = (o * lax.rsqrt(ms + SUBLN_EPS) * g_ref[...]) * (1.0 - lam_init)
        gate = gate_ref[pl.ds(r0, tr), :].astype(F32)
        o_ref[pl.ds(r0, tr), :] = (y * gate).astype(o_ref.dtype)

    for c in range(n_q):
        @pl.when(qi == c)
        def _(c=c):
            chains = [(j, r * tr, c * tq + (r + 1) * tr)
                      for r in range(tq // tr) for j in range(2)]
            chains = chains[::-1]
            s_next = logits(*chains[0])
            outs = {}
            for n, (j, r0, n_keys) in enumerate(chains):
                s = s_next
                if n + 1 < len(chains):
                    s_next = logits(*chains[n + 1])
                p, l = softmax(s, n_keys)
                outs[j] = jnp.dot(p, v_ref[0:n_keys, :], preferred_element_type=F32) / l
                if len(outs) == 2:
                    finish(r0, outs)
                    outs = {}


def _diff_attention(proj, gate_col, lam_vecs, subln_g, *, batch, seq, lam_init,
                    tq=2048, tr=256):
    t_total = proj.shape[0]
    n_q = seq // tq
    assert gate_col % DIFF_V == 0
    gate_block = gate_col // DIFF_V
    slope_cols, pos_cols = _alibi_tables(seq)
    vec_spec = pl.BlockSpec((1, DIFF_QK), lambda b, h, qi: (0, 0))
    lam_vecs = [v.reshape(1, DIFF_QK).astype(F32) for v in lam_vecs]
    return pl.pallas_call(
        functools.partial(_diff_attn_kernel, tq=tq, tr=tr, n_q=n_q, lam_init=lam_init),
        grid=(batch, DIFF_HEADS, n_q),
        in_specs=[
            vec_spec, vec_spec, vec_spec, vec_spec,
            pl.BlockSpec((1, 1, DIFF_QK), lambda b, h, qi: (h, 0, 0)),
            pl.BlockSpec((seq, DIFF_QK), lambda b, h, qi: (0, 0)),
            pl.BlockSpec((tq, DIFF_V), lambda b, h, qi: (b * n_q + qi, h)),
            pl.BlockSpec((seq, DIFF_V), lambda b, h, qi: (b, DIFF_HEADS + h)),
            pl.BlockSpec((seq, DIFF_V), lambda b, h, qi: (b, 2 * DIFF_HEADS + h)),
            pl.BlockSpec((tq, DIFF_V), lambda b, h, qi: (b * n_q + qi, gate_block + h)),
            pl.BlockSpec((1, DIFF_V), lambda b, h, qi: (0, 0)),
        ],
        out_specs=pl.BlockSpec((tq, DIFF_V), lambda b, h, qi: (b * n_q + qi, h)),
        out_shape=jax.ShapeDtypeStruct((t_total, D_MIX), BF16),
        scratch_shapes=[pltpu.VMEM((2, seq, 2 * DIFF_QK), BF16)],
        compiler_params=_params("parallel", "parallel", "arbitrary"),
        name="diff_attention",
    )(*lam_vecs, slope_cols, pos_cols, proj, proj, proj, proj,
      subln_g.reshape(1, DIFF_V).astype(F32))


def _mem_attn_kernel(q_ref, k_ref, v_ref, gate_ref, o_ref):
    scale = MEM_HEAD_DIM ** -0.5
    for h in range(MEM_HEADS):
        sl = slice(h * MEM_HEAD_DIM, (h + 1) * MEM_HEAD_DIM)
        s = lax.dot_general(q_ref[:, sl], k_ref[:, sl], (((1,), (1,)), ((), ())),
                            preferred_element_type=F32) * scale
        m = jnp.max(s, axis=-1, keepdims=True)
        p = jnp.exp(s - m)
        l = jnp.sum(p, axis=-1, keepdims=True)
        o = jnp.dot(p.astype(BF16), v_ref[:, sl], preferred_element_type=F32) / l
        o_ref[:, sl] = (o * gate_ref[:, sl].astype(F32)).astype(o_ref.dtype)


def _mem_attention(proj, q_col, gate_col, kv, *, batch, seq, mem_len, tq=512):
    t_total = proj.shape[0]
    n_q = seq // tq
    assert q_col % D_MEM == 0 and gate_col % D_MEM == 0
    q_block, gate_block = q_col // D_MEM, gate_col // D_MEM
    return pl.pallas_call(
        _mem_attn_kernel,
        grid=(batch, n_q),
        in_specs=[
            pl.BlockSpec((tq, D_MEM), lambda b, qi: (b * n_q + qi, q_block)),
            pl.BlockSpec((mem_len, D_MEM), lambda b, qi: (b, 0)),
            pl.BlockSpec((mem_len, D_MEM), lambda b, qi: (b, 1)),
            pl.BlockSpec((tq, D_MEM), lambda b, qi: (b * n_q + qi, gate_block)),
        ],
        out_specs=pl.BlockSpec((tq, D_MEM), lambda b, qi: (b * n_q + qi, 0)),
        out_shape=jax.ShapeDtypeStruct((t_total, D_MEM), BF16),
        compiler_params=_params("parallel", "arbitrary"),
        name="mem_attention",
    )(proj, kv, kv, proj)


KV_BN = 512


def _lambda_init(layer_idx):
    return 0.8 - 0.6 * math.exp(-0.3 * layer_idx)


def kernel(x, mem, l0_norm_g, l0_w_in, l0_pool_w, l0_pool_scale, l0_mem_norm_g,
           l0_w_mem_kv, l0_w_out, l1_norm_g, l1_w_in, l1_lambda_q1, l1_lambda_k1,
           l1_lambda_q2, l1_lambda_k2, l1_subln_g, l1_mem_norm_g, l1_w_mem_kv,
           l1_w_out, final_norm_g):
    batch, seq, d = x.shape
    mem_len = mem.shape[1]
    x2 = x.reshape(batch * seq, d)
    mem2 = mem.reshape(batch * mem_len, d)
    dims = dict(batch=batch, seq=seq)

    kv_segment = [(0, 2 * D_MEM, PANEL_PLAIN)]

    h = _rmsnorm(x2, l0_norm_g, BF16)
    proj, w_out = _projection(
        h, l0_w_in,
        [(0, D_MIX, PANEL_PLAIN), (D_MIX + D_MEM, D_MIX, PANEL_GATE),
         (D_MIX, D_MEM, PANEL_PLAIN), (2 * D_MIX + D_MEM, D_MEM, PANEL_GATE)],
        also_cast=l0_w_out)
    kv = _projection(_rmsnorm(mem2, l0_mem_norm_g, BF16), l0_w_mem_kv, kv_segment, bn=KV_BN)
    y_mix = _pool_mixer(proj, D_MIX, l0_pool_w, l0_pool_scale, **dims)
    y_mem = _mem_attention(proj, 2 * D_MIX, 2 * D_MIX + D_MEM, kv, mem_len=mem_len, **dims)
    x2, xg, ssq = _out_proj(y_mix, y_mem, w_out, x2, l1_norm_g)

    proj, w_out = _projection(
        xg, l1_w_in,
        [(0, D_MIX, PANEL_SCALED), (D_MIX, 2 * D_MIX, PANEL_PLAIN),
         (3 * D_MIX + D_MEM, D_MIX, PANEL_GATE), (3 * D_MIX, D_MEM, PANEL_PLAIN),
         (4 * D_MIX + D_MEM, D_MEM, PANEL_GATE)],
        col_scale=DIFF_QK ** -0.5, row_ssq=ssq, also_cast=l1_w_out)
    kv = _projection(_rmsnorm(mem2, l1_mem_norm_g, BF16), l1_w_mem_kv, kv_segment, bn=KV_BN)
    y_mix = _diff_attention(proj, 3 * D_MIX,
                            (l1_lambda_q1, l1_lambda_k1, l1_lambda_q2, l1_lambda_k2),
                            l1_subln_g, lam_init=_lambda_init(1), **dims)
    y_mem = _mem_attention(proj, 4 * D_MIX, 4 * D_MIX + D_MEM, kv, mem_len=mem_len, **dims)
    x2 = _out_proj(y_mix, y_mem, w_out, x2)

    return _rmsnorm(x2, final_norm_g, F32).reshape(batch, seq, d)
```

```python
import functools
import math

import numpy as np
import jax
import jax.numpy as jnp
from jax import lax
from jax.experimental import pallas as pl
from jax.experimental.pallas import tpu as pltpu

F32 = jnp.float32
BF16 = jnp.bfloat16

D_MODEL = 4096
D_INNER = 2 * D_MODEL
D_MEM = D_INNER // 4
D_MIX = D_INNER - D_MEM
MEM_HEADS = 4
MEM_HEAD_DIM = D_MEM // MEM_HEADS
POOL_WINDOWS = (2, 4, 8, 16)
POOL_GROUP = D_MIX // len(POOL_WINDOWS)
POOL_HALO = 16
DIFF_QK = 128
DIFF_V = 2 * DIFF_QK
DIFF_HEADS = D_MIX // DIFF_V
RMS_EPS = 1e-6
SUBLN_EPS = 1e-5
MASK_VALUE = -1e30

V7X_VMEM_BYTES = 64 * 1024 * 1024
VMEM_LIMIT_BYTES = V7X_VMEM_BYTES - 8 * 1024 * 1024


def _params(*semantics):
    return pltpu.CompilerParams(dimension_semantics=semantics,
                                vmem_limit_bytes=VMEM_LIMIT_BYTES)


def _silu(z):
    half = 0.5 * z
    return half + half * jnp.tanh(half)


def _alibi_slopes(n):
    def pow2(m):
        start = 2.0 ** (-8.0 / m)
        return [start ** (i + 1) for i in range(m)]
    if math.log2(n).is_integer():
        s = pow2(n)
    else:
        c = 2 ** math.floor(math.log2(n))
        s = pow2(c) + pow2(2 * c)[0::2][: n - c]
    return np.asarray(s, dtype=np.float32)


POS_RADIX = 64
SLOPE_PIECES = 3


def _alibi_tables(seq):
    rest = _alibi_slopes(DIFF_HEADS).astype(np.float64)
    pieces = []
    for _ in range(SLOPE_PIECES):
        piece = rest.astype(BF16).astype(np.float64)
        pieces.append(piece)
        rest = rest - piece
    assert not rest.any(), "slope pieces must be exact"
    n = SLOPE_PIECES
    slope_cols = np.zeros((DIFF_HEADS, 1, DIFF_QK), np.float64)
    slope_cols[:, 0, 0:n] = POS_RADIX * np.stack(pieces, axis=-1)
    slope_cols[:, 0, n:2 * n] = np.stack(pieces, axis=-1)
    pos = np.arange(seq)
    pos_cols = np.zeros((seq, DIFF_QK), np.float64)
    pos_cols[:, 0:n] = (pos // POS_RADIX)[:, None]
    pos_cols[:, n:2 * n] = (pos % POS_RADIX)[:, None]
    return jnp.asarray(slope_cols, BF16), jnp.asarray(pos_cols, BF16)


def _rmsnorm_kernel(x_ref, g_ref, o_ref, *, eps):
    x = x_ref[...].astype(F32)
    ms = jnp.mean(x * x, axis=-1, keepdims=True)
    o_ref[...] = (x * lax.rsqrt(ms + eps) * g_ref[...]).astype(o_ref.dtype)


def _rmsnorm(x, g, out_dtype, *, eps=RMS_EPS, tm=512):
    m, d = x.shape
    return pl.pallas_call(
        functools.partial(_rmsnorm_kernel, eps=eps),
        grid=(m // tm,),
        in_specs=[pl.BlockSpec((tm, d), lambda i: (i, 0)),
                  pl.BlockSpec((1, d), lambda i: (0, 0))],
        out_specs=pl.BlockSpec((tm, d), lambda i: (i, 0)),
        out_shape=jax.ShapeDtypeStruct((m, d), out_dtype),
        compiler_params=_params("parallel"),
        name="rmsnorm",
    )(x, g.reshape(1, d).astype(F32))


def _matmul_kernel(*refs, gate, col0, bn, n_panels, scaled_cols, col_scale, cast_rows):
    if cast_rows:
        (a_ref, w_hbm, src_hbm, o_ref, dst_hbm, stage_ref, wb_ref, sem,
         cast_in, cast_out, cast_sems) = refs
    else:
        a_ref, w_hbm, o_ref, stage_ref, wb_ref, sem = refs
    j = pl.program_id(0)

    if cast_rows:
        step = j * pl.num_programs(1) + pl.program_id(1)
        n_steps = n_panels * pl.num_programs(1)

        def fetch(t):
            return pltpu.make_async_copy(src_hbm.at[pl.ds(t * cast_rows, cast_rows), :],
                                         cast_in, cast_sems.at[0])

        def put(t):
            return pltpu.make_async_copy(cast_out,
                                         dst_hbm.at[pl.ds(t * cast_rows, cast_rows), :],
                                         cast_sems.at[1])

        @pl.when(step == 0)
        def _():
            fetch(0).start()

        fetch(step).wait()

        @pl.when(step > 0)
        def _():
            put(step - 1).wait()

        cast_out[...] = cast_in[...].astype(BF16)
        put(step).start()

        @pl.when(step + 1 < n_steps)
        def _():
            fetch(step + 1).start()

    def panel_copy(jj):
        return pltpu.make_async_copy(w_hbm.at[:, pl.ds(col0 + jj * bn, bn)], stage_ref, sem)

    @pl.when(pl.program_id(1) == 0)
    def _():
        @pl.when(j == 0)
        def _():
            panel_copy(0).start()

        panel_copy(j).wait()
        wb_ref[...] = stage_ref[...].astype(BF16)

        @pl.when(j + 1 < n_panels)
        def _():
            panel_copy(j + 1).start()

    acc = jnp.dot(a_ref[...], wb_ref[...], preferred_element_type=F32)
    if gate:
        acc = _silu(acc)
    if scaled_cols:
        acc = acc * jnp.where(j < scaled_cols // bn, col_scale, 1.0)
    o_ref[...] = acc.astype(o_ref.dtype)

    if cast_rows:
        @pl.when(step == n_steps - 1)
        def _():
            put(step).wait()


def _matmul(a, w, col0, ncols, out_dtype=BF16, *, gate=False, scaled_cols=0, col_scale=1.0,
            also_cast=None, bm=1024, bn=1024):
    m, k = a.shape
    bm = min(bm, m)
    assert ncols % bn == 0 and m % bm == 0 and scaled_cols % bn == 0
    n_panels = ncols // bn
    grid = (n_panels, m // bm)
    in_specs = [pl.BlockSpec((bm, k), lambda j, i: (i, 0)), pl.BlockSpec(memory_space=pl.ANY)]
    out_specs = [pl.BlockSpec((bm, bn), lambda j, i: (i, j))]
    out_shape = [jax.ShapeDtypeStruct((m, ncols), out_dtype)]
    scratch = [pltpu.VMEM((k, bn), F32), pltpu.VMEM((k, bn), BF16), pltpu.SemaphoreType.DMA(())]
    args = [a, w]
    cast_rows = 0
    if also_cast is not None:
        rows, cols = also_cast.shape
        cast_rows = rows // (grid[0] * grid[1])
        assert cast_rows * grid[0] * grid[1] == rows
        in_specs.append(pl.BlockSpec(memory_space=pl.ANY))
        out_specs.append(pl.BlockSpec(memory_space=pl.ANY))
        out_shape.append(jax.ShapeDtypeStruct((rows, cols), BF16))
        scratch += [pltpu.VMEM((cast_rows, cols), F32), pltpu.VMEM((cast_rows, cols), BF16),
                    pltpu.SemaphoreType.DMA((2,))]
        args.append(also_cast)
    out = pl.pallas_call(
        functools.partial(_matmul_kernel, gate=gate, col0=col0, bn=bn, n_panels=n_panels,
                          scaled_cols=scaled_cols, col_scale=col_scale, cast_rows=cast_rows),
        grid=grid,
        in_specs=in_specs,
        out_specs=out_specs,
        out_shape=out_shape,
        scratch_shapes=scratch,
        compiler_params=_params("arbitrary", "arbitrary"),
        name="matmul",
    )(*args)
    return out if also_cast is not None else out[0]


KV_BK = 512


def _kv_proj_kernel(a_ref, w_hbm, o_ref, stage_ref, wb_ref, acc_ref, sem, *, n_slabs):
    kk = pl.program_id(0)

    def slab_copy(t):
        return pltpu.make_async_copy(w_hbm.at[pl.ds(t * KV_BK, KV_BK), :], stage_ref, sem)

    @pl.when(kk == 0)
    def _():
        slab_copy(0).start()

    slab_copy(kk).wait()
    wb_ref[...] = stage_ref[...].astype(BF16)

    @pl.when(kk + 1 < n_slabs)
    def _():
        slab_copy(kk + 1).start()

    part = jnp.dot(a_ref[...], wb_ref[...], preferred_element_type=F32)

    @pl.when(kk == 0)
    def _():
        acc_ref[...] = part

    @pl.when(kk > 0)
    def _():
        acc_ref[...] += part

    @pl.when(kk == n_slabs - 1)
    def _():
        o_ref[...] = acc_ref[...].astype(o_ref.dtype)


def _kv_proj(a, w):
    m, k = a.shape
    n = w.shape[1]
    n_slabs = k // KV_BK
    return pl.pallas_call(
        functools.partial(_kv_proj_kernel, n_slabs=n_slabs),
        grid=(n_slabs,),
        in_specs=[pl.BlockSpec((m, KV_BK), lambda kk: (0, kk)),
                  pl.BlockSpec(memory_space=pl.ANY)],
        out_specs=pl.BlockSpec((m, n), lambda kk: (0, 0)),
        out_shape=jax.ShapeDtypeStruct((m, n), BF16),
        scratch_shapes=[pltpu.VMEM((KV_BK, n), F32), pltpu.VMEM((KV_BK, n), BF16),
                        pltpu.VMEM((m, n), F32), pltpu.SemaphoreType.DMA(())],
        compiler_params=_params("arbitrary"),
        name="kv_proj",
    )(a, w)


OUT_BK = D_MEM
OUT_MIX_CHUNKS = D_MIX // OUT_BK


def _out_proj_kernel(ymix_ref, ymem_ref, w_ref, x_ref, o_ref):
    kk = pl.program_id(2)

    @pl.when(kk == 0)
    def _():
        o_ref[...] = x_ref[...] + jnp.dot(ymix_ref[...], w_ref[...],
                                          preferred_element_type=F32)

    @pl.when(jnp.logical_and(kk > 0, kk < OUT_MIX_CHUNKS))
    def _():
        o_ref[...] += jnp.dot(ymix_ref[...], w_ref[...],
                              preferred_element_type=F32)

    @pl.when(kk == OUT_MIX_CHUNKS)
    def _():
        o_ref[...] += jnp.dot(ymem_ref[...], w_ref[...],
                              preferred_element_type=F32)


def _out_proj(ymix, ymem, w, x, *, bm=1024, bn=1024):
    m = x.shape[0]
    n = w.shape[1]
    last = OUT_MIX_CHUNKS - 1
    return pl.pallas_call(
        _out_proj_kernel,
        grid=(m // bm, n // bn, OUT_MIX_CHUNKS + 1),
        in_specs=[
            pl.BlockSpec((bm, OUT_BK), lambda i, j, k: (i, jnp.minimum(k, last))),
            pl.BlockSpec((bm, OUT_BK), lambda i, j, k: (i, 0)),
            pl.BlockSpec((OUT_BK, bn), lambda i, j, k: (k, j)),
            pl.BlockSpec((bm, bn), lambda i, j, k: (i, j)),
        ],
        out_specs=pl.BlockSpec((bm, bn), lambda i, j, k: (i, j)),
        out_shape=jax.ShapeDtypeStruct((m, n), F32),
        compiler_params=_params("parallel", "parallel", "arbitrary"),
        name="out_proj",
    )(ymix, ymem, w, x)


def _pool_kernel(u_ref, prev_ref, gate_ref, w_ref, scale_ref, o_ref, wb_ref, *, ts, tr):
    g = pl.program_id(0)
    si = pl.program_id(2)

    @pl.when(jnp.logical_and(pl.program_id(1) == 0, si == 0))
    def _():
        wb_ref[...] = w_ref[0].astype(BF16)

    for idx, window in enumerate(POOL_WINDOWS):
        @pl.when(g == idx)
        def _(window=window):
            _pool_tile(u_ref, prev_ref, gate_ref, scale_ref, o_ref, wb_ref, si,
                       ts=ts, tr=tr, window=window)


def _pool_tile(u_ref, prev_ref, gate_ref, scale_ref, o_ref, wb_ref, si, *, ts, tr, window):
    def pooled(rc):
        r0 = rc * tr
        if rc == 0:
            prev = jnp.where(si > 0, prev_ref[...].astype(F32), 0.0)
            xs = jnp.concatenate([prev, u_ref[0:tr, :].astype(F32)], axis=0)
        else:
            xs = u_ref[r0 - POOL_HALO:r0 + tr, :].astype(F32)
        s = xs
        k = 1
        while k < window:
            s = s + pltpu.roll(s, k, 0)
            k *= 2
        t = si * ts + r0 + lax.broadcasted_iota(jnp.int32, (tr, 1), 0)
        inv_cnt = 1.0 / jnp.minimum(t + 1, window).astype(F32)
        return (s[POOL_HALO:] * inv_cnt - xs[POOL_HALO:]).astype(BF16)

    n_sub = ts // tr
    nxt = pooled(0)
    for rc in range(n_sub):
        cur = nxt
        if rc + 1 < n_sub:
            nxt = pooled(rc + 1)
        rows = pl.ds(rc * tr, tr)
        mixed = jnp.dot(cur, wb_ref[...], preferred_element_type=F32)
        o_ref[rows, :] = (mixed * scale_ref[...] * gate_ref[rows, :].astype(F32)
                          ).astype(o_ref.dtype)


def _pool_mixer(u, gate, pool_w, pool_scale, *, batch, seq, ts=512, tr=256):
    t_total = u.shape[0]
    c = POOL_GROUP
    n_s = seq // ts
    halo_blocks_per_tile = ts // POOL_HALO

    def prev_map(g, b, si):
        first = (b * n_s + si) * halo_blocks_per_tile
        return (jnp.maximum(first - 1, 0), g)

    tile = pl.BlockSpec((ts, c), lambda g, b, si: (b * n_s + si, g))
    return pl.pallas_call(
        functools.partial(_pool_kernel, ts=ts, tr=tr),
        grid=(len(POOL_WINDOWS), batch, n_s),
        in_specs=[tile, pl.BlockSpec((POOL_HALO, c), prev_map), tile,
                  pl.BlockSpec((1, c, c), lambda g, b, si: (g, 0, 0)),
                  pl.BlockSpec((1, c), lambda g, b, si: (0, g))],
        out_specs=tile,
        out_shape=jax.ShapeDtypeStruct((t_total, D_MIX), BF16),
        scratch_shapes=[pltpu.VMEM((c, c), BF16)],
        compiler_params=_params("arbitrary", "arbitrary", "arbitrary"),
        name="pool_mixer",
    )(u, u, gate, pool_w, pool_scale.reshape(1, D_MIX).astype(F32))


def _diff_attn_kernel(lq1_ref, lk1_ref, lq2_ref, lk2_ref, slope_ref, pos_ref,
                      q_ref, k_ref, v_ref, gate_ref, g_ref, o_ref, kaug_ref,
                      *, tq, tr, n_q, lam_init):
    qi = pl.program_id(2)

    f = lambda ref: ref[...].astype(F32)
    lam = (jnp.exp(jnp.sum(f(lq1_ref) * f(lk1_ref), keepdims=True))
           - jnp.exp(jnp.sum(f(lq2_ref) * f(lk2_ref), keepdims=True)) + lam_init)

    @pl.when(qi == 0)
    def _():
        for j in range(2):
            kaug_ref[j, :, 0:DIFF_QK] = k_ref[:, j * DIFF_QK:(j + 1) * DIFF_QK]
            kaug_ref[j, :, DIFF_QK:] = pos_ref[...]

    slope_cols = jnp.broadcast_to(slope_ref[0], (tr, DIFF_QK))

    def logits(j, r0, n_keys):
        sl = slice(j * DIFF_QK, (j + 1) * DIFF_QK)
        q_aug = jnp.concatenate([q_ref[pl.ds(r0, tr), sl], slope_cols], axis=-1)
        return lax.dot_general(q_aug, kaug_ref[j, 0:n_keys, :],
                               (((1,), (1,)), ((), ())), preferred_element_type=F32)

    def softmax(s, n_keys):
        n_past = n_keys - tr
        causal = (lax.broadcasted_iota(jnp.int32, (1, tr), 1)
                  <= lax.broadcasted_iota(jnp.int32, (tr, 1), 0))
        own = jnp.where(causal, s[:, n_past:], MASK_VALUE)
        m = jnp.max(own, axis=-1, keepdims=True)
        if n_past:
            past = s[:, :n_past]
            m = jnp.maximum(m, jnp.max(past, axis=-1, keepdims=True))
        p = jnp.exp(own - m)
        if n_past:
            p = jnp.concatenate([jnp.exp(past - m), p], axis=-1)
        return p.astype(BF16), jnp.sum(p, axis=-1, keepdims=True)

    def finish(r0, outs):
        o = outs[0] - lam * outs[1]
        ms = jnp.mean(o * o, axis=-1, keepdims=True)
        y = (o * lax.rsqrt(ms + SUBLN_EPS) * g_ref[...]) * (1.0 - lam_init)
        gate = gate_ref[pl.ds(r0, tr), :].astype(F32)
        o_ref[pl.ds(r0, tr), :] = (y * gate).astype(o_ref.dtype)

    for c in range(n_q):
        @pl.when(qi == c)
        def _(c=c):
            chains = [(j, r * tr, c * tq + (r + 1) * tr)
                      for r in range(tq // tr) for j in range(2)]
            chains = chains[::-1]
            s_next = logits(*chains[0])
            outs = {}
            for n, (j, r0, n_keys) in enumerate(chains):
                s = s_next
                if n + 1 < len(chains):
                    s_next = logits(*chains[n + 1])
                p, l = softmax(s, n_keys)
                outs[j] = jnp.dot(p, v_ref[0:n_keys, :], preferred_element_type=F32) / l
                if len(outs) == 2:
                    finish(r0, outs)
                    outs = {}


def _diff_attention(qkv, gate, lam_vecs, subln_g, *, batch, seq, lam_init, tq=2048, tr=256):
    t_total = qkv.shape[0]
    n_q = seq // tq
    slope_cols, pos_cols = _alibi_tables(seq)
    vec_spec = pl.BlockSpec((1, DIFF_QK), lambda b, h, qi: (0, 0))
    lam_vecs = [v.reshape(1, DIFF_QK).astype(F32) for v in lam_vecs]
    return pl.pallas_call(
        functools.partial(_diff_attn_kernel, tq=tq, tr=tr, n_q=n_q, lam_init=lam_init),
        grid=(batch, DIFF_HEADS, n_q),
        in_specs=[
            vec_spec, vec_spec, vec_spec, vec_spec,
            pl.BlockSpec((1, 1, DIFF_QK), lambda b, h, qi: (h, 0, 0)),
            pl.BlockSpec((seq, DIFF_QK), lambda b, h, qi: (0, 0)),
            pl.BlockSpec((tq, DIFF_V), lambda b, h, qi: (b * n_q + qi, h)),
            pl.BlockSpec((seq, DIFF_V), lambda b, h, qi: (b, DIFF_HEADS + h)),
            pl.BlockSpec((seq, DIFF_V), lambda b, h, qi: (b, 2 * DIFF_HEADS + h)),
            pl.BlockSpec((tq, DIFF_V), lambda b, h, qi: (b * n_q + qi, h)),
            pl.BlockSpec((1, DIFF_V), lambda b, h, qi: (0, 0)),
        ],
        out_specs=pl.BlockSpec((tq, DIFF_V), lambda b, h, qi: (b * n_q + qi, h)),
        out_shape=jax.ShapeDtypeStruct((t_total, D_MIX), BF16),
        scratch_shapes=[pltpu.VMEM((2, seq, 2 * DIFF_QK), BF16)],
        compiler_params=_params("parallel", "parallel", "arbitrary"),
        name="diff_attention",
    )(*lam_vecs, slope_cols, pos_cols, qkv, qkv, qkv, gate,
      subln_g.reshape(1, DIFF_V).astype(F32))


def _mem_attn_kernel(q_ref, k_ref, v_ref, gate_ref, o_ref, *, tr):
    scale = MEM_HEAD_DIM ** -0.5
    chains = [(r * tr, h) for r in range(q_ref.shape[0] // tr) for h in range(MEM_HEADS)]

    def logits(r0, h):
        sl = slice(h * MEM_HEAD_DIM, (h + 1) * MEM_HEAD_DIM)
        return lax.dot_general(q_ref[pl.ds(r0, tr), sl], k_ref[:, sl],
                               (((1,), (1,)), ((), ())), preferred_element_type=F32)

    s_next = logits(*chains[0])
    for n, (r0, h) in enumerate(chains):
        s = s_next * scale
        if n + 1 < len(chains):
            s_next = logits(*chains[n + 1])
        sl = slice(h * MEM_HEAD_DIM, (h + 1) * MEM_HEAD_DIM)
        m = jnp.max(s, axis=-1, keepdims=True)
        p = jnp.exp(s - m)
        l = jnp.sum(p, axis=-1, keepdims=True)
        o = jnp.dot(p.astype(BF16), v_ref[:, sl], preferred_element_type=F32) / l
        rows = pl.ds(r0, tr)
        o_ref[rows, sl] = (o * gate_ref[rows, sl].astype(F32)).astype(o_ref.dtype)


def _mem_attention(qm, kv, gate, *, batch, seq, mem_len, tq=1024, tr=512):
    t_total = qm.shape[0]
    n_q = seq // tq
    z_block = D_MIX // D_MEM
    return pl.pallas_call(
        functools.partial(_mem_attn_kernel, tr=tr),
        grid=(batch, n_q),
        in_specs=[
            pl.BlockSpec((tq, D_MEM), lambda b, qi: (b * n_q + qi, 0)),
            pl.BlockSpec((mem_len, D_MEM), lambda b, qi: (b, 0)),
            pl.BlockSpec((mem_len, D_MEM), lambda b, qi: (b, 1)),
            pl.BlockSpec((tq, D_MEM), lambda b, qi: (b * n_q + qi, z_block)),
        ],
        out_specs=pl.BlockSpec((tq, D_MEM), lambda b, qi: (b * n_q + qi, 0)),
        out_shape=jax.ShapeDtypeStruct((t_total, D_MEM), BF16),
        compiler_params=_params("parallel", "arbitrary"),
        name="mem_attention",
    )(qm, kv, kv, gate)


def _lambda_init(layer_idx):
    return 0.8 - 0.6 * math.exp(-0.3 * layer_idx)


def kernel(x, mem, l0_norm_g, l0_w_in, l0_pool_w, l0_pool_scale, l0_mem_norm_g,
           l0_w_mem_kv, l0_w_out, l1_norm_g, l1_w_in, l1_lambda_q1, l1_lambda_k1,
           l1_lambda_q2, l1_lambda_k2, l1_subln_g, l1_mem_norm_g, l1_w_mem_kv,
           l1_w_out, final_norm_g):
    batch, seq, d = x.shape
    mem_len = mem.shape[1]
    x2 = x.reshape(batch * seq, d)
    mem2 = mem.reshape(batch * mem_len, d)
    dims = dict(batch=batch, seq=seq)

    h = _rmsnorm(x2, l0_norm_g, BF16)
    u = _matmul(h, l0_w_in, 0, D_MIX)
    qm = _matmul(h, l0_w_in, D_MIX, D_MEM)
    gate, w_out = _matmul(h, l0_w_in, D_MIX + D_MEM, D_INNER, gate=True, also_cast=l0_w_out)
    kv = _kv_proj(_rmsnorm(mem2, l0_mem_norm_g, BF16), l0_w_mem_kv)
    y_mix = _pool_mixer(u, gate, l0_pool_w, l0_pool_scale, **dims)
    y_mem = _mem_attention(qm, kv, gate, mem_len=mem_len, **dims)
    x2 = _out_proj(y_mix, y_mem, w_out, x2)

    h = _rmsnorm(x2, l1_norm_g, BF16)
    qkv = _matmul(h, l1_w_in, 0, 3 * D_MIX, scaled_cols=D_MIX, col_scale=DIFF_QK ** -0.5)
    qm = _matmul(h, l1_w_in, 3 * D_MIX, D_MEM)
    gate, w_out = _matmul(h, l1_w_in, 3 * D_MIX + D_MEM, D_INNER, gate=True,
                          also_cast=l1_w_out)
    kv = _kv_proj(_rmsnorm(mem2, l1_mem_norm_g, BF16), l1_w_mem_kv)
    y_mix = _diff_attention(qkv, gate, (l1_lambda_q1, l1_lambda_k1, l1_lambda_q2, l1_lambda_k2),
                            l1_subln_g, lam_init=_lambda_init(1), **dims)
    y_mem = _mem_attention(qm, kv, gate, mem_len=mem_len, **dims)
    x2 = _out_proj(y_mix, y_mem, w_out, x2)

    return _rmsnorm(x2, final_norm_g, F32).reshape(batch, seq, d)
```

```python
import functools
import math

import numpy as np
import jax
import jax.numpy as jnp
from jax import lax
from jax.experimental import pallas as pl
from jax.experimental.pallas import tpu as pltpu

F32 = jnp.float32
BF16 = jnp.bfloat16

D_MODEL = 4096
D_INNER = 2 * D_MODEL
D_MEM = D_INNER // 4
D_MIX = D_INNER - D_MEM
MEM_HEADS = 4
MEM_HEAD_DIM = D_MEM // MEM_HEADS
POOL_WINDOWS = (2, 4, 8, 16)
POOL_GROUP = D_MIX // len(POOL_WINDOWS)
POOL_HALO = 16
DIFF_QK = 128
DIFF_V = 2 * DIFF_QK
DIFF_HEADS = D_MIX // DIFF_V
RMS_EPS = 1e-6
SUBLN_EPS = 1e-5
MASK_VALUE = -1e30

V7X_VMEM_BYTES = 64 * 1024 * 1024
VMEM_LIMIT_BYTES = V7X_VMEM_BYTES - 4 * 1024 * 1024


def _params(*semantics):
    return pltpu.CompilerParams(dimension_semantics=semantics,
                                vmem_limit_bytes=VMEM_LIMIT_BYTES)


def _silu(z):
    half = 0.5 * z
    return half + half * jnp.tanh(half)


def _alibi_slopes(n):
    def pow2(m):
        start = 2.0 ** (-8.0 / m)
        return [start ** (i + 1) for i in range(m)]
    if math.log2(n).is_integer():
        s = pow2(n)
    else:
        c = 2 ** math.floor(math.log2(n))
        s = pow2(c) + pow2(2 * c)[0::2][: n - c]
    return np.asarray(s, dtype=np.float32)


POS_RADIX = 64
SLOPE_PIECES = 3


def _alibi_tables(seq):
    rest = _alibi_slopes(DIFF_HEADS).astype(np.float64)
    pieces = []
    for _ in range(SLOPE_PIECES):
        piece = rest.astype(BF16).astype(np.float64)
        pieces.append(piece)
        rest = rest - piece
    assert not rest.any(), "slope pieces must be exact"
    n = SLOPE_PIECES
    slope_cols = np.zeros((DIFF_HEADS, 1, DIFF_QK), np.float64)
    slope_cols[:, 0, 0:n] = POS_RADIX * np.stack(pieces, axis=-1)
    slope_cols[:, 0, n:2 * n] = np.stack(pieces, axis=-1)
    pos = np.arange(seq)
    pos_cols = np.zeros((seq, DIFF_QK), np.float64)
    pos_cols[:, 0:n] = (pos // POS_RADIX)[:, None]
    pos_cols[:, n:2 * n] = (pos % POS_RADIX)[:, None]
    return jnp.asarray(slope_cols, BF16), jnp.asarray(pos_cols, BF16)


def _rmsnorm_kernel(x_ref, g_ref, o_ref, *, eps):
    x = x_ref[...].astype(F32)
    ms = jnp.mean(x * x, axis=-1, keepdims=True)
    o_ref[...] = (x * lax.rsqrt(ms + eps) * g_ref[...]).astype(o_ref.dtype)


def _rmsnorm(x, g, out_dtype, *, eps=RMS_EPS, tm=512):
    m, d = x.shape
    return pl.pallas_call(
        functools.partial(_rmsnorm_kernel, eps=eps),
        grid=(m // tm,),
        in_specs=[pl.BlockSpec((tm, d), lambda i: (i, 0)),
                  pl.BlockSpec((1, d), lambda i: (0, 0))],
        out_specs=pl.BlockSpec((tm, d), lambda i: (i, 0)),
        out_shape=jax.ShapeDtypeStruct((m, d), out_dtype),
        compiler_params=_params("parallel"),
        name="rmsnorm",
    )(x, g.reshape(1, d).astype(F32))


def _matmul_kernel(*refs, gate, col0, bn, n_panels, scaled_cols, col_scale, cast_rows):
    if cast_rows:
        (a_ref, w_hbm, src_hbm, o_ref, dst_hbm, stage_ref, wb_ref, sem,
         cast_in, cast_out, cast_sems) = refs
    else:
        a_ref, w_hbm, o_ref, stage_ref, wb_ref, sem = refs
    j = pl.program_id(0)

    if cast_rows:
        step = j * pl.num_programs(1) + pl.program_id(1)
        n_steps = n_panels * pl.num_programs(1)

        def fetch(t):
            return pltpu.make_async_copy(src_hbm.at[pl.ds(t * cast_rows, cast_rows), :],
                                         cast_in, cast_sems.at[0])

        def put(t):
            return pltpu.make_async_copy(cast_out,
                                         dst_hbm.at[pl.ds(t * cast_rows, cast_rows), :],
                                         cast_sems.at[1])

        @pl.when(step == 0)
        def _():
            fetch(0).start()

        fetch(step).wait()

        @pl.when(step > 0)
        def _():
            put(step - 1).wait()

        cast_out[...] = cast_in[...].astype(BF16)
        put(step).start()

        @pl.when(step + 1 < n_steps)
        def _():
            fetch(step + 1).start()

    def panel_copy(jj):
        return pltpu.make_async_copy(w_hbm.at[:, pl.ds(col0 + jj * bn, bn)], stage_ref, sem)

    @pl.when(pl.program_id(1) == 0)
    def _():
        @pl.when(j == 0)
        def _():
            panel_copy(0).start()

        panel_copy(j).wait()
        wb_ref[...] = stage_ref[...].astype(BF16)

        @pl.when(j + 1 < n_panels)
        def _():
            panel_copy(j + 1).start()

    acc = jnp.dot(a_ref[...], wb_ref[...], preferred_element_type=F32)
    if gate:
        acc = _silu(acc)
    if scaled_cols:
        acc = acc * jnp.where(j < scaled_cols // bn, col_scale, 1.0)
    o_ref[...] = acc.astype(o_ref.dtype)

    if cast_rows:
        @pl.when(step == n_steps - 1)
        def _():
            put(step).wait()


def _matmul(a, w, col0, ncols, out_dtype=BF16, *, gate=False, scaled_cols=0, col_scale=1.0,
            also_cast=None, bm=1024, bn=1024):
    m, k = a.shape
    bm = min(bm, m)
    assert ncols % bn == 0 and m % bm == 0 and scaled_cols % bn == 0
    n_panels = ncols // bn
    grid = (n_panels, m // bm)
    in_specs = [pl.BlockSpec((bm, k), lambda j, i: (i, 0)), pl.BlockSpec(memory_space=pl.ANY)]
    out_specs = [pl.BlockSpec((bm, bn), lambda j, i: (i, j))]
    out_shape = [jax.ShapeDtypeStruct((m, ncols), out_dtype)]
    scratch = [pltpu.VMEM((k, bn), F32), pltpu.VMEM((k, bn), BF16), pltpu.SemaphoreType.DMA(())]
    args = [a, w]
    cast_rows = 0
    if also_cast is not None:
        rows, cols = also_cast.shape
        cast_rows = rows // (grid[0] * grid[1])
        assert cast_rows * grid[0] * grid[1] == rows
        in_specs.append(pl.BlockSpec(memory_space=pl.ANY))
        out_specs.append(pl.BlockSpec(memory_space=pl.ANY))
        out_shape.append(jax.ShapeDtypeStruct((rows, cols), BF16))
        scratch += [pltpu.VMEM((cast_rows, cols), F32), pltpu.VMEM((cast_rows, cols), BF16),
                    pltpu.SemaphoreType.DMA((2,))]
        args.append(also_cast)
    out = pl.pallas_call(
        functools.partial(_matmul_kernel, gate=gate, col0=col0, bn=bn, n_panels=n_panels,
                          scaled_cols=scaled_cols, col_scale=col_scale, cast_rows=cast_rows),
        grid=grid,
        in_specs=in_specs,
        out_specs=out_specs,
        out_shape=out_shape,
        scratch_shapes=scratch,
        compiler_params=_params("arbitrary", "arbitrary"),
        name="matmul",
    )(*args)
    return out if also_cast is not None else out[0]


KV_BK = 512


def _kv_proj_kernel(a_ref, w_hbm, o_ref, stage_ref, wb_ref, acc_ref, sem, *, n_slabs):
    kk = pl.program_id(0)

    def slab_copy(t):
        return pltpu.make_async_copy(w_hbm.at[pl.ds(t * KV_BK, KV_BK), :], stage_ref, sem)

    @pl.when(kk == 0)
    def _():
        slab_copy(0).start()

    slab_copy(kk).wait()
    wb_ref[...] = stage_ref[...].astype(BF16)

    @pl.when(kk + 1 < n_slabs)
    def _():
        slab_copy(kk + 1).start()

    part = jnp.dot(a_ref[...], wb_ref[...], preferred_element_type=F32)

    @pl.when(kk == 0)
    def _():
        acc_ref[...] = part

    @pl.when(kk > 0)
    def _():
        acc_ref[...] += part

    @pl.when(kk == n_slabs - 1)
    def _():
        o_ref[...] = acc_ref[...].astype(o_ref.dtype)


def _kv_proj(a, w):
    m, k = a.shape
    n = w.shape[1]
    n_slabs = k // KV_BK
    return pl.pallas_call(
        functools.partial(_kv_proj_kernel, n_slabs=n_slabs),
        grid=(n_slabs,),
        in_specs=[pl.BlockSpec((m, KV_BK), lambda kk: (0, kk)),
                  pl.BlockSpec(memory_space=pl.ANY)],
        out_specs=pl.BlockSpec((m, n), lambda kk: (0, 0)),
        out_shape=jax.ShapeDtypeStruct((m, n), BF16),
        scratch_shapes=[pltpu.VMEM((KV_BK, n), F32), pltpu.VMEM((KV_BK, n), BF16),
                        pltpu.VMEM((m, n), F32), pltpu.SemaphoreType.DMA(())],
        compiler_params=_params("arbitrary"),
        name="kv_proj",
    )(a, w)


def _out_proj_kernel(ymix_ref, ymem_ref, w_ref, x_ref, o_ref):
    o_ref[...] = (x_ref[...]
                  + jnp.dot(ymix_ref[...], w_ref[0:D_MIX, :], preferred_element_type=F32)
                  + jnp.dot(ymem_ref[...], w_ref[D_MIX:, :], preferred_element_type=F32))


def _out_proj(ymix, ymem, w, x, *, bm=1024, bn=512):
    m = x.shape[0]
    k, n = w.shape
    return pl.pallas_call(
        _out_proj_kernel,
        grid=(m // bm, n // bn),
        in_specs=[
            pl.BlockSpec((bm, D_MIX), lambda i, j: (i, 0)),
            pl.BlockSpec((bm, D_MEM), lambda i, j: (i, 0)),
            pl.BlockSpec((k, bn), lambda i, j: (0, j)),
            pl.BlockSpec((bm, bn), lambda i, j: (i, j)),
        ],
        out_specs=pl.BlockSpec((bm, bn), lambda i, j: (i, j)),
        out_shape=jax.ShapeDtypeStruct((m, n), F32),
        compiler_params=_params("parallel", "arbitrary"),
        name="out_proj",
    )(ymix, ymem, w, x)


def _pool_kernel(u_ref, prev_ref, gate_ref, w_ref, scale_ref, o_ref, wb_ref, *, ts, tr):
    g = pl.program_id(0)
    si = pl.program_id(2)

    @pl.when(jnp.logical_and(pl.program_id(1) == 0, si == 0))
    def _():
        wb_ref[...] = w_ref[0].astype(BF16)

    for idx, window in enumerate(POOL_WINDOWS):
        @pl.when(g == idx)
        def _(window=window):
            _pool_tile(u_ref, prev_ref, gate_ref, scale_ref, o_ref, wb_ref, si,
                       ts=ts, tr=tr, window=window)


def _pool_tile(u_ref, prev_ref, gate_ref, scale_ref, o_ref, wb_ref, si, *, ts, tr, window):
    def pooled(rc):
        r0 = rc * tr
        if rc == 0:
            prev = jnp.where(si > 0, prev_ref[...].astype(F32), 0.0)
            xs = jnp.concatenate([prev, u_ref[0:tr, :].astype(F32)], axis=0)
        else:
            xs = u_ref[r0 - POOL_HALO:r0 + tr, :].astype(F32)
        s = xs
        k = 1
        while k < window:
            s = s + pltpu.roll(s, k, 0)
            k *= 2
        t = si * ts + r0 + lax.broadcasted_iota(jnp.int32, (tr, 1), 0)
        inv_cnt = 1.0 / jnp.minimum(t + 1, window).astype(F32)
        return (s[POOL_HALO:] * inv_cnt - xs[POOL_HALO:]).astype(BF16)

    n_sub = ts // tr
    nxt = pooled(0)
    for rc in range(n_sub):
        cur = nxt
        if rc + 1 < n_sub:
            nxt = pooled(rc + 1)
        rows = pl.ds(rc * tr, tr)
        mixed = jnp.dot(cur, wb_ref[...], preferred_element_type=F32)
        o_ref[rows, :] = (mixed * scale_ref[...] * gate_ref[rows, :].astype(F32)
                          ).astype(o_ref.dtype)


def _pool_mixer(u, gate, pool_w, pool_scale, *, batch, seq, ts=512, tr=256):
    t_total = u.shape[0]
    c = POOL_GROUP
    n_s = seq // ts
    halo_blocks_per_tile = ts // POOL_HALO

    def prev_map(g, b, si):
        first = (b * n_s + si) * halo_blocks_per_tile
        return (jnp.maximum(first - 1, 0), g)

    tile = pl.BlockSpec((ts, c), lambda g, b, si: (b * n_s + si, g))
    return pl.pallas_call(
        functools.partial(_pool_kernel, ts=ts, tr=tr),
        grid=(len(POOL_WINDOWS), batch, n_s),
        in_specs=[tile, pl.BlockSpec((POOL_HALO, c), prev_map), tile,
                  pl.BlockSpec((1, c, c), lambda g, b, si: (g, 0, 0)),
                  pl.BlockSpec((1, c), lambda g, b, si: (0, g))],
        out_specs=tile,
        out_shape=jax.ShapeDtypeStruct((t_total, D_MIX), BF16),
        scratch_shapes=[pltpu.VMEM((c, c), BF16)],
        compiler_params=_params("arbitrary", "arbitrary", "arbitrary"),
        name="pool_mixer",
    )(u, u, gate, pool_w, pool_scale.reshape(1, D_MIX).astype(F32))


def _diff_attn_kernel(lq1_ref, lk1_ref, lq2_ref, lk2_ref, slope_ref, pos_ref,
                      q_ref, k_ref, v_ref, gate_ref, g_ref, o_ref, kaug_ref,
                      *, tq, tr, n_q, lam_init):
    qi = pl.program_id(2)

    f = lambda ref: ref[...].astype(F32)
    lam = (jnp.exp(jnp.sum(f(lq1_ref) * f(lk1_ref), keepdims=True))
           - jnp.exp(jnp.sum(f(lq2_ref) * f(lk2_ref), keepdims=True)) + lam_init)

    @pl.when(qi == 0)
    def _():
        for j in range(2):
            kaug_ref[j, :, 0:DIFF_QK] = k_ref[:, j * DIFF_QK:(j + 1) * DIFF_QK]
            kaug_ref[j, :, DIFF_QK:] = pos_ref[...]

    slope_cols = jnp.broadcast_to(slope_ref[0], (tr, DIFF_QK))

    def logits(j, r0, n_keys):
        sl = slice(j * DIFF_QK, (j + 1) * DIFF_QK)
        q_aug = jnp.concatenate([q_ref[pl.ds(r0, tr), sl], slope_cols], axis=-1)
        return lax.dot_general(q_aug, kaug_ref[j, 0:n_keys, :],
                               (((1,), (1,)), ((), ())), preferred_element_type=F32)

    def softmax(s, n_keys):
        n_past = n_keys - tr
        causal = (lax.broadcasted_iota(jnp.int32, (1, tr), 1)
                  <= lax.broadcasted_iota(jnp.int32, (tr, 1), 0))
        own = jnp.where(causal, s[:, n_past:], MASK_VALUE)
        m = jnp.max(own, axis=-1, keepdims=True)
        if n_past:
            past = s[:, :n_past]
            m = jnp.maximum(m, jnp.max(past, axis=-1, keepdims=True))
        p = jnp.exp(own - m)
        if n_past:
            p = jnp.concatenate([jnp.exp(past - m), p], axis=-1)
        return p.astype(BF16), jnp.sum(p, axis=-1, keepdims=True)

    def finish(r0, outs):
        o = outs[0] - lam * outs[1]
        ms = jnp.mean(o * o, axis=-1, keepdims=True)
        y = (o * lax.rsqrt(ms + SUBLN_EPS) * g_ref[...]) * (1.0 - lam_init)
        gate = gate_ref[pl.ds(r0, tr), :].astype(F32)
        o_ref[pl.ds(r0, tr), :] = (y * gate).astype(o_ref.dtype)

    for c in range(n_q):
        @pl.when(qi == c)
        def _(c=c):
            chains = [(j, r * tr, c * tq + (r + 1) * tr)
                      for r in range(tq // tr) for j in range(2)]
            chains = chains[::-1]
            s_next = logits(*chains[0])
            outs = {}
            for n, (j, r0, n_keys) in enumerate(chains):
                s = s_next
                if n + 1 < len(chains):
                    s_next = logits(*chains[n + 1])
                p, l = softmax(s, n_keys)
                outs[j] = jnp.dot(p, v_ref[0:n_keys, :], preferred_element_type=F32) / l
                if len(outs) == 2:
                    finish(r0, outs)
                    outs = {}


def _diff_attention(qkv, gate, lam_vecs, subln_g, *, batch, seq, lam_init, tq=2048, tr=256):
    t_total = qkv.shape[0]
    n_q = seq // tq
    slope_cols, pos_cols = _alibi_tables(seq)
    vec_spec = pl.BlockSpec((1, DIFF_QK), lambda b, h, qi: (0, 0))
    lam_vecs = [v.reshape(1, DIFF_QK).astype(F32) for v in lam_vecs]
    return pl.pallas_call(
        functools.partial(_diff_attn_kernel, tq=tq, tr=tr, n_q=n_q, lam_init=lam_init),
        grid=(batch, DIFF_HEADS, n_q),
        in_specs=[
            vec_spec, vec_spec, vec_spec, vec_spec,
            pl.BlockSpec((1, 1, DIFF_QK), lambda b, h, qi: (h, 0, 0)),
            pl.BlockSpec((seq, DIFF_QK), lambda b, h, qi: (0, 0)),
            pl.BlockSpec((tq, DIFF_V), lambda b, h, qi: (b * n_q + qi, h)),
            pl.BlockSpec((seq, DIFF_V), lambda b, h, qi: (b, DIFF_HEADS + h)),
            pl.BlockSpec((seq, DIFF_V), lambda b, h, qi: (b, 2 * DIFF_HEADS + h)),
            pl.BlockSpec((tq, DIFF_V), lambda b, h, qi: (b * n_q + qi, h)),
            pl.BlockSpec((1, DIFF_V), lambda b, h, qi: (0, 0)),
        ],
        out_specs=pl.BlockSpec((tq, DIFF_V), lambda b, h, qi: (b * n_q + qi, h)),
        out_shape=jax.ShapeDtypeStruct((t_total, D_MIX), BF16),
        scratch_shapes=[pltpu.VMEM((2, seq, 2 * DIFF_QK), BF16)],
        compiler_params=_params("parallel", "parallel", "arbitrary"),
        name="diff_attention",
    )(*lam_vecs, slope_cols, pos_cols, qkv, qkv, qkv, gate,
      subln_g.reshape(1, DIFF_V).astype(F32))


def _mem_attn_kernel(q_ref, k_ref, v_ref, gate_ref, o_ref, *, tr):
    scale = MEM_HEAD_DIM ** -0.5
    chains = [(r * tr, h) for r in range(q_ref.shape[0] // tr) for h in range(MEM_HEADS)]

    def logits(r0, h):
        sl = slice(h * MEM_HEAD_DIM, (h + 1) * MEM_HEAD_DIM)
        return lax.dot_general(q_ref[pl.ds(r0, tr), sl], k_ref[:, sl],
                               (((1,), (1,)), ((), ())), preferred_element_type=F32)

    s_next = logits(*chains[0])
    for n, (r0, h) in enumerate(chains):
        s = s_next * scale
        if n + 1 < len(chains):
            s_next = logits(*chains[n + 1])
        sl = slice(h * MEM_HEAD_DIM, (h + 1) * MEM_HEAD_DIM)
        m = jnp.max(s, axis=-1, keepdims=True)
        p = jnp.exp(s - m)
        l = jnp.sum(p, axis=-1, keepdims=True)
        o = jnp.dot(p.astype(BF16), v_ref[:, sl], preferred_element_type=F32) / l
        rows = pl.ds(r0, tr)
        o_ref[rows, sl] = (o * gate_ref[rows, sl].astype(F32)).astype(o_ref.dtype)


def _mem_attention(qm, kv, gate, *, batch, seq, mem_len, tq=1024, tr=512):
    t_total = qm.shape[0]
    n_q = seq // tq
    z_block = D_MIX // D_MEM
    return pl.pallas_call(
        functools.partial(_mem_attn_kernel, tr=tr),
        grid=(batch, n_q),
        in_specs=[
            pl.BlockSpec((tq, D_MEM), lambda b, qi: (b * n_q + qi, 0)),
            pl.BlockSpec((mem_len, D_MEM), lambda b, qi: (b, 0)),
            pl.BlockSpec((mem_len, D_MEM), lambda b, qi: (b, 1)),
            pl.BlockSpec((tq, D_MEM), lambda b, qi: (b * n_q + qi, z_block)),
        ],
        out_specs=pl.BlockSpec((tq, D_MEM), lambda b, qi: (b * n_q + qi, 0)),
        out_shape=jax.ShapeDtypeStruct((t_total, D_MEM), BF16),
        compiler_params=_params("parallel", "arbitrary"),
        name="mem_attention",
    )(qm, kv, kv, gate)


def _lambda_init(layer_idx):
    return 0.8 - 0.6 * math.exp(-0.3 * layer_idx)


def kernel(x, mem, l0_norm_g, l0_w_in, l0_pool_w, l0_pool_scale, l0_mem_norm_g,
           l0_w_mem_kv, l0_w_out, l1_norm_g, l1_w_in, l1_lambda_q1, l1_lambda_k1,
           l1_lambda_q2, l1_lambda_k2, l1_subln_g, l1_mem_norm_g, l1_w_mem_kv,
           l1_w_out, final_norm_g):
    batch, seq, d = x.shape
    mem_len = mem.shape[1]
    x2 = x.reshape(batch * seq, d)
    mem2 = mem.reshape(batch * mem_len, d)
    dims = dict(batch=batch, seq=seq)

    h = _rmsnorm(x2, l0_norm_g, BF16)
    u = _matmul(h, l0_w_in, 0, D_MIX)
    qm = _matmul(h, l0_w_in, D_MIX, D_MEM)
    gate, w_out = _matmul(h, l0_w_in, D_MIX + D_MEM, D_INNER, gate=True, also_cast=l0_w_out)
    kv = _kv_proj(_rmsnorm(mem2, l0_mem_norm_g, BF16), l0_w_mem_kv)
    y_mix = _pool_mixer(u, gate, l0_pool_w, l0_pool_scale, **dims)
    y_mem = _mem_attention(qm, kv, gate, mem_len=mem_len, **dims)
    x2 = _out_proj(y_mix, y_mem, w_out, x2)

    h = _rmsnorm(x2, l1_norm_g, BF16)
    qkv = _matmul(h, l1_w_in, 0, 3 * D_MIX, scaled_cols=D_MIX, col_scale=DIFF_QK ** -0.5)
    qm = _matmul(h, l1_w_in, 3 * D_MIX, D_MEM)
    gate, w_out = _matmul(h, l1_w_in, 3 * D_MIX + D_MEM, D_INNER, gate=True,
                          also_cast=l1_w_out)
    kv = _kv_proj(_rmsnorm(mem2, l1_mem_norm_g, BF16), l1_w_mem_kv)
    y_mix = _diff_attention(qkv, gate, (l1_lambda_q1, l1_lambda_k1, l1_lambda_q2, l1_lambda_k2),
                            l1_subln_g, lam_init=_lambda_init(1), **dims)
    y_mem = _mem_attention(qm, kv, gate, mem_len=mem_len, **dims)
    x2 = _out_proj(y_mix, y_mem, w_out, x2)

    return _rmsnorm(x2, final_norm_g, F32).reshape(batch, seq, d)
```

```python
import functools
import math

import numpy as np
import jax
import jax.numpy as jnp
from jax import lax
from jax.experimental import pallas as pl
from jax.experimental.pallas import tpu as pltpu

F32 = jnp.float32
BF16 = jnp.bfloat16

D_MODEL = 4096
D_INNER = 2 * D_MODEL
D_MEM = D_INNER // 4
D_MIX = D_INNER - D_MEM
MEM_HEADS = 4
MEM_HEAD_DIM = D_MEM // MEM_HEADS
POOL_WINDOWS = (2, 4, 8, 16)
POOL_GROUP = D_MIX // len(POOL_WINDOWS)
POOL_HALO = 16
DIFF_QK = 128
DIFF_V = 2 * DIFF_QK
DIFF_HEADS = D_MIX // DIFF_V
RMS_EPS = 1e-6
SUBLN_EPS = 1e-5
MASK_VALUE = -1e30

V7X_VMEM_BYTES = 64 * 1024 * 1024
VMEM_LIMIT_BYTES = V7X_VMEM_BYTES - 4 * 1024 * 1024


def _params(*semantics):
    return pltpu.CompilerParams(dimension_semantics=semantics,
                                vmem_limit_bytes=VMEM_LIMIT_BYTES)


def _silu(z):
    half = 0.5 * z
    return half + half * jnp.tanh(half)


def _alibi_slopes(n):
    def pow2(m):
        start = 2.0 ** (-8.0 / m)
        return [start ** (i + 1) for i in range(m)]
    if math.log2(n).is_integer():
        s = pow2(n)
    else:
        c = 2 ** math.floor(math.log2(n))
        s = pow2(c) + pow2(2 * c)[0::2][: n - c]
    return np.asarray(s, dtype=np.float32)


POS_RADIX = 64
SLOPE_PIECES = 3


def _alibi_tables(seq):
    rest = _alibi_slopes(DIFF_HEADS).astype(np.float64)
    pieces = []
    for _ in range(SLOPE_PIECES):
        piece = rest.astype(BF16).astype(np.float64)
        pieces.append(piece)
        rest = rest - piece
    assert not rest.any(), "slope pieces must be exact"
    n = SLOPE_PIECES
    slope_cols = np.zeros((DIFF_HEADS, 1, DIFF_QK), np.float64)
    slope_cols[:, 0, 0:n] = POS_RADIX * np.stack(pieces, axis=-1)
    slope_cols[:, 0, n:2 * n] = np.stack(pieces, axis=-1)
    pos = np.arange(seq)
    pos_cols = np.zeros((seq, DIFF_QK), np.float64)
    pos_cols[:, 0:n] = (pos // POS_RADIX)[:, None]
    pos_cols[:, n:2 * n] = (pos % POS_RADIX)[:, None]
    return jnp.asarray(slope_cols, BF16), jnp.asarray(pos_cols, BF16)


def _rmsnorm_kernel(x_ref, g_ref, o_ref, *, eps):
    x = x_ref[...].astype(F32)
    ms = jnp.mean(x * x, axis=-1, keepdims=True)
    o_ref[...] = (x * lax.rsqrt(ms + eps) * g_ref[...]).astype(o_ref.dtype)


def _rmsnorm(x, g, out_dtype, *, eps=RMS_EPS):
    m, d = x.shape
    tm = min(512, m // 2)
    return pl.pallas_call(
        functools.partial(_rmsnorm_kernel, eps=eps),
        grid=(m // tm,),
        in_specs=[pl.BlockSpec((tm, d), lambda i: (i, 0)),
                  pl.BlockSpec((1, d), lambda i: (0, 0))],
        out_specs=pl.BlockSpec((tm, d), lambda i: (i, 0)),
        out_shape=jax.ShapeDtypeStruct((m, d), out_dtype),
        compiler_params=_params("parallel"),
        name="rmsnorm",
    )(x, g.reshape(1, d).astype(F32))


def _matmul_kernel(*refs, gate, col0, bn, n_panels, scaled_cols, col_scale, cast_rows):
    if cast_rows:
        (a_ref, w_hbm, src_hbm, o_ref, dst_hbm, stage_ref, wb_ref, sem,
         cast_in, cast_out, cast_sems) = refs
    else:
        a_ref, w_hbm, o_ref, stage_ref, wb_ref, sem = refs
    j = pl.program_id(0)

    if cast_rows:
        step = j * pl.num_programs(1) + pl.program_id(1)
        n_steps = n_panels * pl.num_programs(1)

        def fetch(t):
            return pltpu.make_async_copy(src_hbm.at[pl.ds(t * cast_rows, cast_rows), :],
                                         cast_in, cast_sems.at[0])

        def put(t):
            return pltpu.make_async_copy(cast_out,
                                         dst_hbm.at[pl.ds(t * cast_rows, cast_rows), :],
                                         cast_sems.at[1])

        @pl.when(step == 0)
        def _():
            fetch(0).start()

        fetch(step).wait()

        @pl.when(step > 0)
        def _():
            put(step - 1).wait()

        cast_out[...] = cast_in[...].astype(BF16)
        put(step).start()

        @pl.when(step + 1 < n_steps)
        def _():
            fetch(step + 1).start()

    def panel_copy(jj):
        return pltpu.make_async_copy(w_hbm.at[:, pl.ds(col0 + jj * bn, bn)], stage_ref, sem)

    @pl.when(pl.program_id(1) == 0)
    def _():
        @pl.when(j == 0)
        def _():
            panel_copy(0).start()

        panel_copy(j).wait()
        wb_ref[...] = stage_ref[...].astype(BF16)

        @pl.when(j + 1 < n_panels)
        def _():
            panel_copy(j + 1).start()

    acc = jnp.dot(a_ref[...], wb_ref[...], preferred_element_type=F32)
    if gate:
        acc = _silu(acc)
    if scaled_cols:
        acc = acc * jnp.where(j < scaled_cols // bn, col_scale, 1.0)
    o_ref[...] = acc.astype(o_ref.dtype)

    if cast_rows:
        @pl.when(step == n_steps - 1)
        def _():
            put(step).wait()


def _matmul(a, w, col0, ncols, out_dtype=BF16, *, gate=False, scaled_cols=0, col_scale=1.0,
            also_cast=None, bm=1024, bn=1024):
    m, k = a.shape
    bm = min(bm, m)
    assert ncols % bn == 0 and m % bm == 0 and scaled_cols % bn == 0
    n_panels = ncols // bn
    grid = (n_panels, m // bm)
    in_specs = [pl.BlockSpec((bm, k), lambda j, i: (i, 0)), pl.BlockSpec(memory_space=pl.ANY)]
    out_specs = [pl.BlockSpec((bm, bn), lambda j, i: (i, j))]
    out_shape = [jax.ShapeDtypeStruct((m, ncols), out_dtype)]
    scratch = [pltpu.VMEM((k, bn), F32), pltpu.VMEM((k, bn), BF16), pltpu.SemaphoreType.DMA(())]
    args = [a, w]
    cast_rows = 0
    if also_cast is not None:
        rows, cols = also_cast.shape
        cast_rows = rows // (grid[0] * grid[1])
        assert cast_rows * grid[0] * grid[1] == rows
        in_specs.append(pl.BlockSpec(memory_space=pl.ANY))
        out_specs.append(pl.BlockSpec(memory_space=pl.ANY))
        out_shape.append(jax.ShapeDtypeStruct((rows, cols), BF16))
        scratch += [pltpu.VMEM((cast_rows, cols), F32), pltpu.VMEM((cast_rows, cols), BF16),
                    pltpu.SemaphoreType.DMA((2,))]
        args.append(also_cast)
    out = pl.pallas_call(
        functools.partial(_matmul_kernel, gate=gate, col0=col0, bn=bn, n_panels=n_panels,
                          scaled_cols=scaled_cols, col_scale=col_scale, cast_rows=cast_rows),
        grid=grid,
        in_specs=in_specs,
        out_specs=out_specs,
        out_shape=out_shape,
        scratch_shapes=scratch,
        compiler_params=_params("arbitrary", "arbitrary"),
        name="matmul",
    )(*args)
    return out if also_cast is not None else out[0]


KV_BK = 512


def _kv_proj_kernel(a_ref, w_hbm, o_ref, stage_ref, wb_ref, acc_ref, sem, *, n_slabs):
    kk = pl.program_id(0)

    def slab_copy(t):
        return pltpu.make_async_copy(w_hbm.at[pl.ds(t * KV_BK, KV_BK), :], stage_ref, sem)

    @pl.when(kk == 0)
    def _():
        slab_copy(0).start()

    slab_copy(kk).wait()
    wb_ref[...] = stage_ref[...].astype(BF16)

    @pl.when(kk + 1 < n_slabs)
    def _():
        slab_copy(kk + 1).start()

    part = jnp.dot(a_ref[...], wb_ref[...], preferred_element_type=F32)

    @pl.when(kk == 0)
    def _():
        acc_ref[...] = part

    @pl.when(kk > 0)
    def _():
        acc_ref[...] += part

    @pl.when(kk == n_slabs - 1)
    def _():
        o_ref[...] = acc_ref[...].astype(o_ref.dtype)


def _kv_proj(a, w):
    m, k = a.shape
    n = w.shape[1]
    n_slabs = k // KV_BK
    return pl.pallas_call(
        functools.partial(_kv_proj_kernel, n_slabs=n_slabs),
        grid=(n_slabs,),
        in_specs=[pl.BlockSpec((m, KV_BK), lambda kk: (0, kk)),
                  pl.BlockSpec(memory_space=pl.ANY)],
        out_specs=pl.BlockSpec((m, n), lambda kk: (0, 0)),
        out_shape=jax.ShapeDtypeStruct((m, n), BF16),
        scratch_shapes=[pltpu.VMEM((KV_BK, n), F32), pltpu.VMEM((KV_BK, n), BF16),
                        pltpu.VMEM((m, n), F32), pltpu.SemaphoreType.DMA(())],
        compiler_params=_params("arbitrary"),
        name="kv_proj",
    )(a, w)


def _out_proj_kernel(ymix_ref, ymem_ref, w_ref, x_ref, o_ref):
    o_ref[...] = (x_ref[...]
                  + jnp.dot(ymix_ref[...], w_ref[0:D_MIX, :], preferred_element_type=F32)
                  + jnp.dot(ymem_ref[...], w_ref[D_MIX:, :], preferred_element_type=F32))


def _out_proj(ymix, ymem, w, x, *, bm=1024, bn=512):
    m = x.shape[0]
    k, n = w.shape
    return pl.pallas_call(
        _out_proj_kernel,
        grid=(m // bm, n // bn),
        in_specs=[
            pl.BlockSpec((bm, D_MIX), lambda i, j: (i, 0)),
            pl.BlockSpec((bm, D_MEM), lambda i, j: (i, 0)),
            pl.BlockSpec((k, bn), lambda i, j: (0, j)),
            pl.BlockSpec((bm, bn), lambda i, j: (i, j)),
        ],
        out_specs=pl.BlockSpec((bm, bn), lambda i, j: (i, j)),
        out_shape=jax.ShapeDtypeStruct((m, n), F32),
        compiler_params=_params("parallel", "arbitrary"),
        name="out_proj",
    )(ymix, ymem, w, x)


def _pool_kernel(u_ref, prev_ref, gate_ref, w_ref, scale_ref, o_ref, wb_ref, *, ts, tr):
    g = pl.program_id(0)
    si = pl.program_id(2)

    @pl.when(jnp.logical_and(pl.program_id(1) == 0, si == 0))
    def _():
        wb_ref[...] = w_ref[0].astype(BF16)

    for idx, window in enumerate(POOL_WINDOWS):
        @pl.when(g == idx)
        def _(window=window):
            _pool_tile(u_ref, prev_ref, gate_ref, scale_ref, o_ref, wb_ref, si,
                       ts=ts, tr=tr, window=window)


def _pool_tile(u_ref, prev_ref, gate_ref, scale_ref, o_ref, wb_ref, si, *, ts, tr, window):
    def pooled(rc):
        r0 = rc * tr
        if rc == 0:
            prev = jnp.where(si > 0, prev_ref[...].astype(F32), 0.0)
            xs = jnp.concatenate([prev, u_ref[0:tr, :].astype(F32)], axis=0)
        else:
            xs = u_ref[r0 - POOL_HALO:r0 + tr, :].astype(F32)
        s = xs
        k = 1
        while k < window:
            s = s + pltpu.roll(s, k, 0)
            k *= 2
        t = si * ts + r0 + lax.broadcasted_iota(jnp.int32, (tr, 1), 0)
        inv_cnt = 1.0 / jnp.minimum(t + 1, window).astype(F32)
        return (s[POOL_HALO:] * inv_cnt - xs[POOL_HALO:]).astype(BF16)

    n_sub = ts // tr
    nxt = pooled(0)
    for rc in range(n_sub):
        cur = nxt
        if rc + 1 < n_sub:
            nxt = pooled(rc + 1)
        rows = pl.ds(rc * tr, tr)
        mixed = jnp.dot(cur, wb_ref[...], preferred_element_type=F32)
        o_ref[rows, :] = (mixed * scale_ref[...] * gate_ref[rows, :].astype(F32)
                          ).astype(o_ref.dtype)


def _pool_mixer(u, gate, pool_w, pool_scale, *, batch, seq, ts=1024, tr=256):
    t_total = u.shape[0]
    c = POOL_GROUP
    n_s = seq // ts
    halo_blocks_per_tile = ts // POOL_HALO

    def prev_map(g, b, si):
        first = (b * n_s + si) * halo_blocks_per_tile
        return (jnp.maximum(first - 1, 0), g)

    tile = pl.BlockSpec((ts, c), lambda g, b, si: (b * n_s + si, g))
    return pl.pallas_call(
        functools.partial(_pool_kernel, ts=ts, tr=tr),
        grid=(len(POOL_WINDOWS), batch, n_s),
        in_specs=[tile, pl.BlockSpec((POOL_HALO, c), prev_map), tile,
                  pl.BlockSpec((1, c, c), lambda g, b, si: (g, 0, 0)),
                  pl.BlockSpec((1, c), lambda g, b, si: (0, g))],
        out_specs=tile,
        out_shape=jax.ShapeDtypeStruct((t_total, D_MIX), BF16),
        scratch_shapes=[pltpu.VMEM((c, c), BF16)],
        compiler_params=_params("arbitrary", "arbitrary", "arbitrary"),
        name="pool_mixer",
    )(u, u, gate, pool_w, pool_scale.reshape(1, D_MIX).astype(F32))


def _diff_attn_kernel(lq1_ref, lk1_ref, lq2_ref, lk2_ref, slope_ref, pos_ref,
                      q_ref, k_ref, v_ref, gate_ref, g_ref, o_ref, kaug_ref,
                      *, tq, tr, n_q, lam_init):
    qi = pl.program_id(2)

    f = lambda ref: ref[...].astype(F32)
    lam = (jnp.exp(jnp.sum(f(lq1_ref) * f(lk1_ref), keepdims=True))
           - jnp.exp(jnp.sum(f(lq2_ref) * f(lk2_ref), keepdims=True)) + lam_init)

    @pl.when(qi == 0)
    def _():
        for j in range(2):
            kaug_ref[j, :, 0:DIFF_QK] = k_ref[:, j * DIFF_QK:(j + 1) * DIFF_QK]
            kaug_ref[j, :, DIFF_QK:] = pos_ref[...]

    slope_cols = jnp.broadcast_to(slope_ref[0], (tr, DIFF_QK))

    def logits(j, r0, n_keys):
        sl = slice(j * DIFF_QK, (j + 1) * DIFF_QK)
        q_aug = jnp.concatenate([q_ref[pl.ds(r0, tr), sl], slope_cols], axis=-1)
        return lax.dot_general(q_aug, kaug_ref[j, 0:n_keys, :],
                               (((1,), (1,)), ((), ())), preferred_element_type=F32)

    def softmax(s, n_keys):
        n_past = n_keys - tr
        causal = (lax.broadcasted_iota(jnp.int32, (1, tr), 1)
                  <= lax.broadcasted_iota(jnp.int32, (tr, 1), 0))
        own = jnp.where(causal, s[:, n_past:], MASK_VALUE)
        m = jnp.max(own, axis=-1, keepdims=True)
        if n_past:
            past = s[:, :n_past]
            m = jnp.maximum(m, jnp.max(past, axis=-1, keepdims=True))
        p = jnp.exp(own - m)
        if n_past:
            p = jnp.concatenate([jnp.exp(past - m), p], axis=-1)
        return p.astype(BF16), jnp.sum(p, axis=-1, keepdims=True)

    subln_gain = g_ref[...] * (1.0 - lam_init)

    def finish(r0, pv, l):
        o = pv[0] * (1.0 / l[0]) - pv[1] * (lam / l[1])
        ms = jnp.mean(o * o, axis=-1, keepdims=True)
        y = o * lax.rsqrt(ms + SUBLN_EPS) * subln_gain
        gate = gate_ref[pl.ds(r0, tr), :].astype(F32)
        o_ref[pl.ds(r0, tr), :] = (y * gate).astype(o_ref.dtype)

    for c in range(n_q):
        @pl.when(qi == c)
        def _(c=c):
            chains = [(j, r * tr, c * tq + (r + 1) * tr)
                      for r in range(tq // tr) for j in range(2)]
            chains = chains[::-1]
            s_next = logits(*chains[0])
            pv, ls = {}, {}
            for n, (j, r0, n_keys) in enumerate(chains):
                s = s_next
                if n + 1 < len(chains):
                    s_next = logits(*chains[n + 1])
                p, ls[j] = softmax(s, n_keys)
                pv[j] = jnp.dot(p, v_ref[0:n_keys, :], preferred_element_type=F32)
                if len(pv) == 2:
                    finish(r0, pv, ls)
                    pv, ls = {}, {}


def _diff_attention(qkv, gate, lam_vecs, subln_g, *, batch, seq, lam_init, tq=2048, tr=256):
    t_total = qkv.shape[0]
    n_q = seq // tq
    slope_cols, pos_cols = _alibi_tables(seq)
    vec_spec = pl.BlockSpec((1, DIFF_QK), lambda b, h, qi: (0, 0))
    lam_vecs = [v.reshape(1, DIFF_QK).astype(F32) for v in lam_vecs]
    return pl.pallas_call(
        functools.partial(_diff_attn_kernel, tq=tq, tr=tr, n_q=n_q, lam_init=lam_init),
        grid=(batch, DIFF_HEADS, n_q),
        in_specs=[
            vec_spec, vec_spec, vec_spec, vec_spec,
            pl.BlockSpec((1, 1, DIFF_QK), lambda b, h, qi: (h, 0, 0)),
            pl.BlockSpec((seq, DIFF_QK), lambda b, h, qi: (0, 0)),
            pl.BlockSpec((tq, DIFF_V), lambda b, h, qi: (b * n_q + qi, h)),
            pl.BlockSpec((seq, DIFF_V), lambda b, h, qi: (b, DIFF_HEADS + h)),
            pl.BlockSpec((seq, DIFF_V), lambda b, h, qi: (b, 2 * DIFF_HEADS + h)),
            pl.BlockSpec((tq, DIFF_V), lambda b, h, qi: (b * n_q + qi, h)),
            pl.BlockSpec((1, DIFF_V), lambda b, h, qi: (0, 0)),
        ],
        out_specs=pl.BlockSpec((tq, DIFF_V), lambda b, h, qi: (b * n_q + qi, h)),
        out_shape=jax.ShapeDtypeStruct((t_total, D_MIX), BF16),
        scratch_shapes=[pltpu.VMEM((2, seq, 2 * DIFF_QK), BF16)],
        compiler_params=_params("parallel", "parallel", "arbitrary"),
        name="diff_attention",
    )(*lam_vecs, slope_cols, pos_cols, qkv, qkv, qkv, gate,
      subln_g.reshape(1, DIFF_V).astype(F32))


def _mem_attn_kernel(q_ref, k_ref, v_ref, gate_ref, o_ref, *, tr):
    scale = MEM_HEAD_DIM ** -0.5
    chains = [(r * tr, h) for r in range(q_ref.shape[0] // tr) for h in range(MEM_HEADS)]

    def logits(r0, h):
        sl = slice(h * MEM_HEAD_DIM, (h + 1) * MEM_HEAD_DIM)
        return lax.dot_general(q_ref[pl.ds(r0, tr), sl], k_ref[:, sl],
                               (((1,), (1,)), ((), ())), preferred_element_type=F32)

    s_next = logits(*chains[0])
    for n, (r0, h) in enumerate(chains):
        s = s_next * scale
        if n + 1 < len(chains):
            s_next = logits(*chains[n + 1])
        sl = slice(h * MEM_HEAD_DIM, (h + 1) * MEM_HEAD_DIM)
        m = jnp.max(s, axis=-1, keepdims=True)
        p = jnp.exp(s - m)
        l = jnp.sum(p, axis=-1, keepdims=True)
        o = jnp.dot(p.astype(BF16), v_ref[:, sl], preferred_element_type=F32) / l
        rows = pl.ds(r0, tr)
        o_ref[rows, sl] = (o * gate_ref[rows, sl].astype(F32)).astype(o_ref.dtype)


def _mem_attention(qm, kv, gate, *, batch, seq, mem_len, tq=1024, tr=512):
    t_total = qm.shape[0]
    n_q = seq // tq
    z_block = D_MIX // D_MEM
    return pl.pallas_call(
        functools.partial(_mem_attn_kernel, tr=tr),
        grid=(batch, n_q),
        in_specs=[
            pl.BlockSpec((tq, D_MEM), lambda b, qi: (b * n_q + qi, 0)),
            pl.BlockSpec((mem_len, D_MEM), lambda b, qi: (b, 0)),
            pl.BlockSpec((mem_len, D_MEM), lambda b, qi: (b, 1)),
            pl.BlockSpec((tq, D_MEM), lambda b, qi: (b * n_q + qi, z_block)),
        ],
        out_specs=pl.BlockSpec((tq, D_MEM), lambda b, qi: (b * n_q + qi, 0)),
        out_shape=jax.ShapeDtypeStruct((t_total, D_MEM), BF16),
        compiler_params=_params("parallel", "arbitrary"),
        name="mem_attention",
    )(qm, kv, kv, gate)


def _lambda_init(layer_idx):
    return 0.8 - 0.6 * math.exp(-0.3 * layer_idx)


def kernel(x, mem, l0_norm_g, l0_w_in, l0_pool_w, l0_pool_scale, l0_mem_norm_g,
           l0_w_mem_kv, l0_w_out, l1_norm_g, l1_w_in, l1_lambda_q1, l1_lambda_k1,
           l1_lambda_q2, l1_lambda_k2, l1_subln_g, l1_mem_norm_g, l1_w_mem_kv,
           l1_w_out, final_norm_g):
    batch, seq, d = x.shape
    mem_len = mem.shape[1]
    x2 = x.reshape(batch * seq, d)
    mem2 = mem.reshape(batch * mem_len, d)
    dims = dict(batch=batch, seq=seq)

    h = _rmsnorm(x2, l0_norm_g, BF16)
    u = _matmul(h, l0_w_in, 0, D_MIX)
    qm = _matmul(h, l0_w_in, D_MIX, D_MEM)
    gate, w_out = _matmul(h, l0_w_in, D_MIX + D_MEM, D_INNER, gate=True, also_cast=l0_w_out)
    kv = _kv_proj(_rmsnorm(mem2, l0_mem_norm_g, BF16), l0_w_mem_kv)
    y_mix = _pool_mixer(u, gate, l0_pool_w, l0_pool_scale, **dims)
    y_mem = _mem_attention(qm, kv, gate, mem_len=mem_len, **dims)
    x2 = _out_proj(y_mix, y_mem, w_out, x2)

    h = _rmsnorm(x2, l1_norm_g, BF16)
    qkv = _matmul(h, l1_w_in, 0, 3 * D_MIX, scaled_cols=D_MIX, col_scale=DIFF_QK ** -0.5)
    qm = _matmul(h, l1_w_in, 3 * D_MIX, D_MEM)
    gate, w_out = _matmul(h, l1_w_in, 3 * D_MIX + D_MEM, D_INNER, gate=True,
                          also_cast=l1_w_out)
    kv = _kv_proj(_rmsnorm(mem2, l1_mem_norm_g, BF16), l1_w_mem_kv)
    y_mix = _diff_attention(qkv, gate, (l1_lambda_q1, l1_lambda_k1, l1_lambda_q2, l1_lambda_k2),
                            l1_subln_g, lam_init=_lambda_init(1), **dims)
    y_mem = _mem_attention(qm, kv, gate, mem_len=mem_len, **dims)
    x2 = _out_proj(y_mix, y_mem, w_out, x2)

    return _rmsnorm(x2, final_norm_g, F32).reshape(batch, seq, d)
```

```python
import functools
import math

import numpy as np
import jax
import jax.numpy as jnp
from jax import lax
from jax.experimental import pallas as pl
from jax.experimental.pallas import tpu as pltpu

F32 = jnp.float32
BF16 = jnp.bfloat16

D_MODEL = 4096
D_INNER = 2 * D_MODEL
D_MEM = D_INNER // 4
D_MIX = D_INNER - D_MEM
MEM_HEADS = 4
MEM_HEAD_DIM = D_MEM // MEM_HEADS
POOL_WINDOWS = (2, 4, 8, 16)
POOL_GROUP = D_MIX // len(POOL_WINDOWS)
POOL_HALO = 16
DIFF_QK = 128
DIFF_V = 2 * DIFF_QK
DIFF_HEADS = D_MIX // DIFF_V
RMS_EPS = 1e-6
SUBLN_EPS = 1e-5
MASK_VALUE = -1e30

V7X_VMEM_BYTES = 64 * 1024 * 1024
VMEM_LIMIT_BYTES = V7X_VMEM_BYTES - 4 * 1024 * 1024


def _params(*semantics):
    return pltpu.CompilerParams(dimension_semantics=semantics,
                                vmem_limit_bytes=VMEM_LIMIT_BYTES)


def _silu(z):
    half = 0.5 * z
    return half + half * jnp.tanh(half)


def _alibi_slopes(n):
    def pow2(m):
        start = 2.0 ** (-8.0 / m)
        return [start ** (i + 1) for i in range(m)]
    if math.log2(n).is_integer():
        s = pow2(n)
    else:
        c = 2 ** math.floor(math.log2(n))
        s = pow2(c) + pow2(2 * c)[0::2][: n - c]
    return np.asarray(s, dtype=np.float32)


POS_RADIX = 64
SLOPE_PIECES = 3


def _alibi_tables(seq):
    rest = _alibi_slopes(DIFF_HEADS).astype(np.float64)
    pieces = []
    for _ in range(SLOPE_PIECES):
        piece = rest.astype(BF16).astype(np.float64)
        pieces.append(piece)
        rest = rest - piece
    assert not rest.any(), "slope pieces must be exact"
    n = SLOPE_PIECES
    slope_cols = np.zeros((DIFF_HEADS, 1, DIFF_QK), np.float64)
    slope_cols[:, 0, 0:n] = POS_RADIX * np.stack(pieces, axis=-1)
    slope_cols[:, 0, n:2 * n] = np.stack(pieces, axis=-1)
    pos = np.arange(seq)
    pos_cols = np.zeros((seq, DIFF_QK), np.float64)
    pos_cols[:, 0:n] = (pos // POS_RADIX)[:, None]
    pos_cols[:, n:2 * n] = (pos % POS_RADIX)[:, None]
    return jnp.asarray(slope_cols, BF16), jnp.asarray(pos_cols, BF16)


def _rmsnorm_kernel(x_ref, g_ref, o_ref, *, eps):
    x = x_ref[...].astype(F32)
    ms = jnp.mean(x * x, axis=-1, keepdims=True)
    o_ref[...] = (x * lax.rsqrt(ms + eps) * g_ref[...]).astype(o_ref.dtype)


def _rmsnorm(x, g, out_dtype, *, eps=RMS_EPS):
    m, d = x.shape
    tm = min(512, m // 2)
    return pl.pallas_call(
        functools.partial(_rmsnorm_kernel, eps=eps),
        grid=(m // tm,),
        in_specs=[pl.BlockSpec((tm, d), lambda i: (i, 0)),
                  pl.BlockSpec((1, d), lambda i: (0, 0))],
        out_specs=pl.BlockSpec((tm, d), lambda i: (i, 0)),
        out_shape=jax.ShapeDtypeStruct((m, d), out_dtype),
        compiler_params=_params("parallel"),
        name="rmsnorm",
    )(x, g.reshape(1, d).astype(F32))


def _matmul_kernel(*refs, gate, col0, bn, n_panels, scaled_cols, col_scale, cast_rows):
    if cast_rows:
        (a_ref, w_hbm, src_hbm, o_ref, dst_hbm, stage_ref, wb_ref, sem,
         cast_in, cast_out, cast_sems) = refs
    else:
        a_ref, w_hbm, o_ref, stage_ref, wb_ref, sem = refs
    j = pl.program_id(0)

    if cast_rows:
        step = j * pl.num_programs(1) + pl.program_id(1)
        n_steps = n_panels * pl.num_programs(1)

        def fetch(t):
            return pltpu.make_async_copy(src_hbm.at[pl.ds(t * cast_rows, cast_rows), :],
                                         cast_in, cast_sems.at[0])

        def put(t):
            return pltpu.make_async_copy(cast_out,
                                         dst_hbm.at[pl.ds(t * cast_rows, cast_rows), :],
                                         cast_sems.at[1])

        @pl.when(step == 0)
        def _():
            fetch(0).start()

        fetch(step).wait()

        @pl.when(step > 0)
        def _():
            put(step - 1).wait()

        cast_out[...] = cast_in[...].astype(BF16)
        put(step).start()

        @pl.when(step + 1 < n_steps)
        def _():
            fetch(step + 1).start()

    def panel_copy(jj):
        return pltpu.make_async_copy(w_hbm.at[:, pl.ds(col0 + jj * bn, bn)], stage_ref, sem)

    @pl.when(pl.program_id(1) == 0)
    def _():
        @pl.when(j == 0)
        def _():
            panel_copy(0).start()

        panel_copy(j).wait()
        wb_ref[...] = stage_ref[...].astype(BF16)

        @pl.when(j + 1 < n_panels)
        def _():
            panel_copy(j + 1).start()

    acc = jnp.dot(a_ref[...], wb_ref[...], preferred_element_type=F32)
    if gate:
        acc = _silu(acc)
    if scaled_cols:
        acc = acc * jnp.where(j < scaled_cols // bn, col_scale, 1.0)
    o_ref[...] = acc.astype(o_ref.dtype)

    if cast_rows:
        @pl.when(step == n_steps - 1)
        def _():
            put(step).wait()


def _matmul(a, w, col0, ncols, out_dtype=BF16, *, gate=False, scaled_cols=0, col_scale=1.0,
            also_cast=None, bm=1024, bn=1024):
    m, k = a.shape
    bm = min(bm, m)
    assert ncols % bn == 0 and m % bm == 0 and scaled_cols % bn == 0
    n_panels = ncols // bn
    grid = (n_panels, m // bm)
    in_specs = [pl.BlockSpec((bm, k), lambda j, i: (i, 0)), pl.BlockSpec(memory_space=pl.ANY)]
    out_specs = [pl.BlockSpec((bm, bn), lambda j, i: (i, j))]
    out_shape = [jax.ShapeDtypeStruct((m, ncols), out_dtype)]
    scratch = [pltpu.VMEM((k, bn), F32), pltpu.VMEM((k, bn), BF16), pltpu.SemaphoreType.DMA(())]
    args = [a, w]
    cast_rows = 0
    if also_cast is not None:
        rows, cols = also_cast.shape
        cast_rows = rows // (grid[0] * grid[1])
        assert cast_rows * grid[0] * grid[1] == rows
        in_specs.append(pl.BlockSpec(memory_space=pl.ANY))
        out_specs.append(pl.BlockSpec(memory_space=pl.ANY))
        out_shape.append(jax.ShapeDtypeStruct((rows, cols), BF16))
        scratch += [pltpu.VMEM((cast_rows, cols), F32), pltpu.VMEM((cast_rows, cols), BF16),
                    pltpu.SemaphoreType.DMA((2,))]
        args.append(also_cast)
    out = pl.pallas_call(
        functools.partial(_matmul_kernel, gate=gate, col0=col0, bn=bn, n_panels=n_panels,
                          scaled_cols=scaled_cols, col_scale=col_scale, cast_rows=cast_rows),
        grid=grid,
        in_specs=in_specs,
        out_specs=out_specs,
        out_shape=out_shape,
        scratch_shapes=scratch,
        compiler_params=_params("arbitrary", "arbitrary"),
        name="matmul",
    )(*args)
    return out if also_cast is not None else out[0]


KV_BN = 512


def _kv_proj_kernel(a_ref, w_hbm, o_ref, stage_a, stage_b, wb_a, wb_b, sems, *, n_pairs):
    t = pl.program_id(0)
    last = t == n_pairs - 1

    def copy_a(p):
        return pltpu.make_async_copy(w_hbm.at[:, pl.ds(p * KV_BN, KV_BN)], stage_a, sems.at[0])

    def copy_b(p):
        return pltpu.make_async_copy(w_hbm.at[:, pl.ds(p * KV_BN, KV_BN)], stage_b, sems.at[1])

    @pl.when(t == 0)
    def _():
        copy_a(0).start()
        copy_b(1).start()
        copy_a(0).wait()
        wb_a[...] = stage_a[...].astype(BF16)

        if n_pairs > 1:
            copy_a(2).start()

    copy_b(2 * t + 1).wait()
    wb_b[...] = stage_b[...].astype(BF16)
    o_ref[:, 0:KV_BN] = jnp.dot(a_ref[...], wb_a[...],
                                preferred_element_type=F32).astype(o_ref.dtype)

    @pl.when(jnp.logical_not(last))
    def _():
        copy_b(2 * t + 3).start()
        copy_a(2 * t + 2).wait()
        o_ref[:, KV_BN:] = jnp.dot(a_ref[...], wb_b[...],
                                   preferred_element_type=F32).astype(o_ref.dtype)
        wb_a[...] = stage_a[...].astype(BF16)

        @pl.when(t + 2 < n_pairs)
        def _():
            copy_a(2 * t + 4).start()

    @pl.when(last)
    def _():
        o_ref[:, KV_BN:] = jnp.dot(a_ref[...], wb_b[...],
                                   preferred_element_type=F32).astype(o_ref.dtype)


def _kv_proj(a, w):
    m, k = a.shape
    n = w.shape[1]
    n_pairs = n // (2 * KV_BN)
    assert n_pairs * 2 * KV_BN == n
    return pl.pallas_call(
        functools.partial(_kv_proj_kernel, n_pairs=n_pairs),
        grid=(n_pairs,),
        in_specs=[pl.BlockSpec((m, k), lambda t: (0, 0)),
                  pl.BlockSpec(memory_space=pl.ANY)],
        out_specs=pl.BlockSpec((m, 2 * KV_BN), lambda t: (0, t)),
        out_shape=jax.ShapeDtypeStruct((m, n), BF16),
        scratch_shapes=[pltpu.VMEM((k, KV_BN), F32), pltpu.VMEM((k, KV_BN), F32),
                        pltpu.VMEM((k, KV_BN), BF16), pltpu.VMEM((k, KV_BN), BF16),
                        pltpu.SemaphoreType.DMA((2,))],
        compiler_params=_params("arbitrary"),
        name="kv_proj",
    )(a, w)


def _out_proj_kernel(ymix_ref, ymem_ref, w_ref, x_ref, o_ref):
    o_ref[...] = (x_ref[...]
                  + jnp.dot(ymix_ref[...], w_ref[0:D_MIX, :], preferred_element_type=F32)
                  + jnp.dot(ymem_ref[...], w_ref[D_MIX:, :], preferred_element_type=F32))


def _out_proj(ymix, ymem, w, x, *, bm=1024, bn=512):
    m = x.shape[0]
    k, n = w.shape
    return pl.pallas_call(
        _out_proj_kernel,
        grid=(m // bm, n // bn),
        in_specs=[
            pl.BlockSpec((bm, D_MIX), lambda i, j: (i, 0)),
            pl.BlockSpec((bm, D_MEM), lambda i, j: (i, 0)),
            pl.BlockSpec((k, bn), lambda i, j: (0, j)),
            pl.BlockSpec((bm, bn), lambda i, j: (i, j)),
        ],
        out_specs=pl.BlockSpec((bm, bn), lambda i, j: (i, j)),
        out_shape=jax.ShapeDtypeStruct((m, n), F32),
        compiler_params=_params("parallel", "arbitrary"),
        name="out_proj",
    )(ymix, ymem, w, x)


def _pool_kernel(u_ref, prev_ref, gate_ref, w_ref, scale_ref, o_ref, wb_ref, *, ts, tr):
    g = pl.program_id(0)
    si = pl.program_id(2)

    @pl.when(jnp.logical_and(pl.program_id(1) == 0, si == 0))
    def _():
        wb_ref[...] = w_ref[0].astype(BF16)

    for idx, window in enumerate(POOL_WINDOWS):
        @pl.when(g == idx)
        def _(window=window):
            _pool_tile(u_ref, prev_ref, gate_ref, scale_ref, o_ref, wb_ref, si,
                       ts=ts, tr=tr, window=window)


def _pool_tile(u_ref, prev_ref, gate_ref, scale_ref, o_ref, wb_ref, si, *, ts, tr, window):
    def pooled(rc):
        r0 = rc * tr
        if rc == 0:
            prev = jnp.where(si > 0, prev_ref[...].astype(F32), 0.0)
            xs = jnp.concatenate([prev, u_ref[0:tr, :].astype(F32)], axis=0)
        else:
            xs = u_ref[r0 - POOL_HALO:r0 + tr, :].astype(F32)
        s = xs
        k = 1
        while k < window:
            s = s + pltpu.roll(s, k, 0)
            k *= 2
        t = si * ts + r0 + lax.broadcasted_iota(jnp.int32, (tr, 1), 0)
        inv_cnt = 1.0 / jnp.minimum(t + 1, window).astype(F32)
        return (s[POOL_HALO:] * inv_cnt - xs[POOL_HALO:]).astype(BF16)

    n_sub = ts // tr
    nxt = pooled(0)
    for rc in range(n_sub):
        cur = nxt
        if rc + 1 < n_sub:
            nxt = pooled(rc + 1)
        rows = pl.ds(rc * tr, tr)
        mixed = jnp.dot(cur, wb_ref[...], preferred_element_type=F32)
        o_ref[rows, :] = (mixed * scale_ref[...] * gate_ref[rows, :].astype(F32)
                          ).astype(o_ref.dtype)


def _pool_mixer(u, gate, pool_w, pool_scale, *, batch, seq, ts=1024, tr=256):
    t_total = u.shape[0]
    c = POOL_GROUP
    n_s = seq // ts
    halo_blocks_per_tile = ts // POOL_HALO

    def prev_map(g, b, si):
        first = (b * n_s + si) * halo_blocks_per_tile
        return (jnp.maximum(first - 1, 0), g)

    tile = pl.BlockSpec((ts, c), lambda g, b, si: (b * n_s + si, g))
    return pl.pallas_call(
        functools.partial(_pool_kernel, ts=ts, tr=tr),
        grid=(len(POOL_WINDOWS), batch, n_s),
        in_specs=[tile, pl.BlockSpec((POOL_HALO, c), prev_map), tile,
                  pl.BlockSpec((1, c, c), lambda g, b, si: (g, 0, 0)),
                  pl.BlockSpec((1, c), lambda g, b, si: (0, g))],
        out_specs=tile,
        out_shape=jax.ShapeDtypeStruct((t_total, D_MIX), BF16),
        scratch_shapes=[pltpu.VMEM((c, c), BF16)],
        compiler_params=_params("arbitrary", "arbitrary", "arbitrary"),
        name="pool_mixer",
    )(u, u, gate, pool_w, pool_scale.reshape(1, D_MIX).astype(F32))


def _diff_attn_kernel(lq1_ref, lk1_ref, lq2_ref, lk2_ref, slope_ref, pos_ref,
                      q_ref, k_ref, v_ref, gate_ref, g_ref, o_ref, kaug_ref,
                      *, tq, tr, n_q, lam_init):
    qi = pl.program_id(2)

    f = lambda ref: ref[...].astype(F32)
    lam = (jnp.exp(jnp.sum(f(lq1_ref) * f(lk1_ref), keepdims=True))
           - jnp.exp(jnp.sum(f(lq2_ref) * f(lk2_ref), keepdims=True)) + lam_init)

    @pl.when(qi == 0)
    def _():
        for j in range(2):
            kaug_ref[j, :, 0:DIFF_QK] = k_ref[:, j * DIFF_QK:(j + 1) * DIFF_QK]
            kaug_ref[j, :, DIFF_QK:] = pos_ref[...]

    slope_cols = jnp.broadcast_to(slope_ref[0], (tr, DIFF_QK))

    def logits(j, r0, n_keys):
        sl = slice(j * DIFF_QK, (j + 1) * DIFF_QK)
        q_aug = jnp.concatenate([q_ref[pl.ds(r0, tr), sl], slope_cols], axis=-1)
        return lax.dot_general(q_aug, kaug_ref[j, 0:n_keys, :],
                               (((1,), (1,)), ((), ())), preferred_element_type=F32)

    def softmax(s, n_keys):
        n_past = n_keys - tr
        causal = (lax.broadcasted_iota(jnp.int32, (1, tr), 1)
                  <= lax.broadcasted_iota(jnp.int32, (tr, 1), 0))
        own = jnp.where(causal, s[:, n_past:], MASK_VALUE)
        m = jnp.max(own, axis=-1, keepdims=True)
        if n_past:
            past = s[:, :n_past]
            m = jnp.maximum(m, jnp.max(past, axis=-1, keepdims=True))
        p = jnp.exp(own - m)
        if n_past:
            p = jnp.concatenate([jnp.exp(past - m), p], axis=-1)
        return p.astype(BF16), jnp.sum(p, axis=-1, keepdims=True)

    subln_gain = g_ref[...] * (1.0 - lam_init)

    def finish(r0, pv, l):
        o = pv[0] * (1.0 / l[0]) - pv[1] * (lam / l[1])
        ms = jnp.mean(o * o, axis=-1, keepdims=True)
        y = o * lax.rsqrt(ms + SUBLN_EPS) * subln_gain
        gate = gate_ref[pl.ds(r0, tr), :].astype(F32)
        o_ref[pl.ds(r0, tr), :] = (y * gate).astype(o_ref.dtype)

    for c in range(n_q):
        @pl.when(qi == c)
        def _(c=c):
            chains = [(j, r * tr, c * tq + (r + 1) * tr)
                      for r in range(tq // tr) for j in range(2)]
            chains = chains[::-1]
            s_next = logits(*chains[0])
            pv, ls = {}, {}
            for n, (j, r0, n_keys) in enumerate(chains):
                s = s_next
                if n + 1 < len(chains):
                    s_next = logits(*chains[n + 1])
                p, ls[j] = softmax(s, n_keys)
                pv[j] = jnp.dot(p, v_ref[0:n_keys, :], preferred_element_type=F32)
                if len(pv) == 2:
                    finish(r0, pv, ls)
                    pv, ls = {}, {}


def _diff_attention(qkv, gate, lam_vecs, subln_g, *, batch, seq, lam_init, tq=2048, tr=256):
    t_total = qkv.shape[0]
    n_q = seq // tq
    slope_cols, pos_cols = _alibi_tables(seq)
    vec_spec = pl.BlockSpec((1, DIFF_QK), lambda b, h, qi: (0, 0))
    lam_vecs = [v.reshape(1, DIFF_QK).astype(F32) for v in lam_vecs]
    return pl.pallas_call(
        functools.partial(_diff_attn_kernel, tq=tq, tr=tr, n_q=n_q, lam_init=lam_init),
        grid=(batch, DIFF_HEADS, n_q),
        in_specs=[
            vec_spec, vec_spec, vec_spec, vec_spec,
            pl.BlockSpec((1, 1, DIFF_QK), lambda b, h, qi: (h, 0, 0)),
            pl.BlockSpec((seq, DIFF_QK), lambda b, h, qi: (0, 0)),
            pl.BlockSpec((tq, DIFF_V), lambda b, h, qi: (b * n_q + qi, h)),
            pl.BlockSpec((seq, DIFF_V), lambda b, h, qi: (b, DIFF_HEADS + h)),
            pl.BlockSpec((seq, DIFF_V), lambda b, h, qi: (b, 2 * DIFF_HEADS + h)),
            pl.BlockSpec((tq, DIFF_V), lambda b, h, qi: (b * n_q + qi, h)),
            pl.BlockSpec((1, DIFF_V), lambda b, h, qi: (0, 0)),
        ],
        out_specs=pl.BlockSpec((tq, DIFF_V), lambda b, h, qi: (b * n_q + qi, h)),
        out_shape=jax.ShapeDtypeStruct((t_total, D_MIX), BF16),
        scratch_shapes=[pltpu.VMEM((2, seq, 2 * DIFF_QK), BF16)],
        compiler_params=_params("parallel", "parallel", "arbitrary"),
        name="diff_attention",
    )(*lam_vecs, slope_cols, pos_cols, qkv, qkv, qkv, gate,
      subln_g.reshape(1, DIFF_V).astype(F32))


def _mem_attn_kernel(q_ref, k_ref, v_ref, gate_ref, o_ref, *, tr):
    scale = MEM_HEAD_DIM ** -0.5
    chains = [(r * tr, h) for r in range(q_ref.shape[0] // tr) for h in range(MEM_HEADS)]

    def logits(r0, h):
        sl = slice(h * MEM_HEAD_DIM, (h + 1) * MEM_HEAD_DIM)
        return lax.dot_general(q_ref[pl.ds(r0, tr), sl], k_ref[:, sl],
                               (((1,), (1,)), ((), ())), preferred_element_type=F32)

    s_next = logits(*chains[0])
    for n, (r0, h) in enumerate(chains):
        s = s_next * scale
        if n + 1 < len(chains):
            s_next = logits(*chains[n + 1])
        sl = slice(h * MEM_HEAD_DIM, (h + 1) * MEM_HEAD_DIM)
        m = jnp.max(s, axis=-1, keepdims=True)
        p = jnp.exp(s - m)
        l = jnp.sum(p, axis=-1, keepdims=True)
        o = jnp.dot(p.astype(BF16), v_ref[:, sl], preferred_element_type=F32) / l
        rows = pl.ds(r0, tr)
        o_ref[rows, sl] = (o * gate_ref[rows, sl].astype(F32)).astype(o_ref.dtype)


def _mem_attention(qm, kv, gate, *, batch, seq, mem_len, tq=1024, tr=512):
    t_total = qm.shape[0]
    n_q = seq // tq
    z_block = D_MIX // D_MEM
    return pl.pallas_call(
        functools.partial(_mem_attn_kernel, tr=tr),
        grid=(batch, n_q),
        in_specs=[
            pl.BlockSpec((tq, D_MEM), lambda b, qi: (b * n_q + qi, 0)),
            pl.BlockSpec((mem_len, D_MEM), lambda b, qi: (b, 0)),
            pl.BlockSpec((mem_len, D_MEM), lambda b, qi: (b, 1)),
            pl.BlockSpec((tq, D_MEM), lambda b, qi: (b * n_q + qi, z_block)),
        ],
        out_specs=pl.BlockSpec((tq, D_MEM), lambda b, qi: (b * n_q + qi, 0)),
        out_shape=jax.ShapeDtypeStruct((t_total, D_MEM), BF16),
        compiler_params=_params("parallel", "arbitrary"),
        name="mem_attention",
    )(qm, kv, kv, gate)


def _lambda_init(layer_idx):
    return 0.8 - 0.6 * math.exp(-0.3 * layer_idx)


def kernel(x, mem, l0_norm_g, l0_w_in, l0_pool_w, l0_pool_scale, l0_mem_norm_g,
           l0_w_mem_kv, l0_w_out, l1_norm_g, l1_w_in, l1_lambda_q1, l1_lambda_k1,
           l1_lambda_q2, l1_lambda_k2, l1_subln_g, l1_mem_norm_g, l1_w_mem_kv,
           l1_w_out, final_norm_g):
    batch, seq, d = x.shape
    mem_len = mem.shape[1]
    x2 = x.reshape(batch * seq, d)
    mem2 = mem.reshape(batch * mem_len, d)
    dims = dict(batch=batch, seq=seq)

    h = _rmsnorm(x2, l0_norm_g, BF16)
    u = _matmul(h, l0_w_in, 0, D_MIX)
    qm = _matmul(h, l0_w_in, D_MIX, D_MEM)
    gate, w_out = _matmul(h, l0_w_in, D_MIX + D_MEM, D_INNER, gate=True, also_cast=l0_w_out)
    kv = _kv_proj(_rmsnorm(mem2, l0_mem_norm_g, BF16), l0_w_mem_kv)
    y_mix = _pool_mixer(u, gate, l0_pool_w, l0_pool_scale, **dims)
    y_mem = _mem_attention(qm, kv, gate, mem_len=mem_len, **dims)
    x2 = _out_proj(y_mix, y_mem, w_out, x2)

    h = _rmsnorm(x2, l1_norm_g, BF16)
    qkv = _matmul(h, l1_w_in, 0, 3 * D_MIX, scaled_cols=D_MIX, col_scale=DIFF_QK ** -0.5)
    qm = _matmul(h, l1_w_in, 3 * D_MIX, D_MEM)
    gate, w_out = _matmul(h, l1_w_in, 3 * D_MIX + D_MEM, D_INNER, gate=True,
                          also_cast=l1_w_out)
    kv = _kv_proj(_rmsnorm(mem2, l1_mem_norm_g, BF16), l1_w_mem_kv)
    y_mix = _diff_attention(qkv, gate, (l1_lambda_q1, l1_lambda_k1, l1_lambda_q2, l1_lambda_k2),
                            l1_subln_g, lam_init=_lambda_init(1), **dims)
    y_mem = _mem_attention(qm, kv, gate, mem_len=mem_len, **dims)
    x2 = _out_proj(y_mix, y_mem, w_out, x2)

    return _rmsnorm(x2, final_norm_g, F32).reshape(batch, seq, d)
```

```python
import functools
import math

import numpy as np
import jax
import jax.numpy as jnp
from jax import lax
from jax.experimental import pallas as pl
from jax.experimental.pallas import tpu as pltpu

F32 = jnp.float32
BF16 = jnp.bfloat16

D_MODEL = 4096
D_INNER = 2 * D_MODEL
D_MEM = D_INNER // 4
D_MIX = D_INNER - D_MEM
MEM_HEADS = 4
MEM_HEAD_DIM = D_MEM // MEM_HEADS
POOL_WINDOWS = (2, 4, 8, 16)
POOL_GROUP = D_MIX // len(POOL_WINDOWS)
POOL_HALO = 16
DIFF_QK = 128
DIFF_V = 2 * DIFF_QK
DIFF_HEADS = D_MIX // DIFF_V
RMS_EPS = 1e-6
SUBLN_EPS = 1e-5
MASK_VALUE = -1e30

V7X_VMEM_BYTES = 64 * 1024 * 1024
VMEM_LIMIT_BYTES = V7X_VMEM_BYTES - 4 * 1024 * 1024


def _params(*semantics):
    return pltpu.CompilerParams(dimension_semantics=semantics,
                                vmem_limit_bytes=VMEM_LIMIT_BYTES)


def _silu(z):
    half = 0.5 * z
    return half + half * jnp.tanh(half)


def _alibi_slopes(n):
    def pow2(m):
        start = 2.0 ** (-8.0 / m)
        return [start ** (i + 1) for i in range(m)]
    if math.log2(n).is_integer():
        s = pow2(n)
    else:
        c = 2 ** math.floor(math.log2(n))
        s = pow2(c) + pow2(2 * c)[0::2][: n - c]
    return np.asarray(s, dtype=np.float32)


POS_RADIX = 64
SLOPE_PIECES = 3


def _alibi_tables(seq):
    rest = _alibi_slopes(DIFF_HEADS).astype(np.float64)
    pieces = []
    for _ in range(SLOPE_PIECES):
        piece = rest.astype(BF16).astype(np.float64)
        pieces.append(piece)
        rest = rest - piece
    assert not rest.any(), "slope pieces must be exact"
    n = SLOPE_PIECES
    slope_cols = np.zeros((DIFF_HEADS, 1, DIFF_QK), np.float64)
    slope_cols[:, 0, 0:n] = POS_RADIX * np.stack(pieces, axis=-1)
    slope_cols[:, 0, n:2 * n] = np.stack(pieces, axis=-1)
    pos = np.arange(seq)
    pos_cols = np.zeros((seq, DIFF_QK), np.float64)
    pos_cols[:, 0:n] = (pos // POS_RADIX)[:, None]
    pos_cols[:, n:2 * n] = (pos % POS_RADIX)[:, None]
    return jnp.asarray(slope_cols, BF16), jnp.asarray(pos_cols, BF16)


def _rmsnorm_kernel(x_ref, g_ref, o_ref, *, eps):
    x = x_ref[...].astype(F32)
    ms = jnp.mean(x * x, axis=-1, keepdims=True)
    o_ref[...] = (x * lax.rsqrt(ms + eps) * g_ref[...]).astype(o_ref.dtype)


def _rmsnorm(x, g, out_dtype, *, eps=RMS_EPS):
    m, d = x.shape
    tm = min(512, m // 2)
    return pl.pallas_call(
        functools.partial(_rmsnorm_kernel, eps=eps),
        grid=(m // tm,),
        in_specs=[pl.BlockSpec((tm, d), lambda i: (i, 0)),
                  pl.BlockSpec((1, d), lambda i: (0, 0))],
        out_specs=pl.BlockSpec((tm, d), lambda i: (i, 0)),
        out_shape=jax.ShapeDtypeStruct((m, d), out_dtype),
        compiler_params=_params("parallel"),
        name="rmsnorm",
    )(x, g.reshape(1, d).astype(F32))


def _matmul_kernel(*refs, gate, col0, bn, n_panels, scaled_cols, col_scale, cast_rows):
    if cast_rows:
        (a_ref, w_hbm, src_hbm, o_ref, dst_hbm, stage_ref, wb_ref, sem,
         cast_in, cast_out, cast_sems) = refs
    else:
        a_ref, w_hbm, o_ref, stage_ref, wb_ref, sem = refs
    j = pl.program_id(0)

    if cast_rows:
        step = j * pl.num_programs(1) + pl.program_id(1)
        n_steps = n_panels * pl.num_programs(1)

        def fetch(t):
            return pltpu.make_async_copy(src_hbm.at[pl.ds(t * cast_rows, cast_rows), :],
                                         cast_in, cast_sems.at[0])

        def put(t):
            return pltpu.make_async_copy(cast_out,
                                         dst_hbm.at[pl.ds(t * cast_rows, cast_rows), :],
                                         cast_sems.at[1])

        @pl.when(step == 0)
        def _():
            fetch(0).start()

        fetch(step).wait()

        @pl.when(step > 0)
        def _():
            put(step - 1).wait()

        cast_out[...] = cast_in[...].astype(BF16)
        put(step).start()

        @pl.when(step + 1 < n_steps)
        def _():
            fetch(step + 1).start()

    def panel_copy(jj):
        return pltpu.make_async_copy(w_hbm.at[:, pl.ds(col0 + jj * bn, bn)], stage_ref, sem)

    @pl.when(pl.program_id(1) == 0)
    def _():
        @pl.when(j == 0)
        def _():
            panel_copy(0).start()

        panel_copy(j).wait()
        wb_ref[...] = stage_ref[...].astype(BF16)

        @pl.when(j + 1 < n_panels)
        def _():
            panel_copy(j + 1).start()

    acc = jnp.dot(a_ref[...], wb_ref[...], preferred_element_type=F32)
    if gate:
        acc = _silu(acc)
    if scaled_cols:
        acc = acc * jnp.where(j < scaled_cols // bn, col_scale, 1.0)
    o_ref[...] = acc.astype(o_ref.dtype)

    if cast_rows:
        @pl.when(step == n_steps - 1)
        def _():
            put(step).wait()


def _matmul(a, w, col0, ncols, out_dtype=BF16, *, gate=False, scaled_cols=0, col_scale=1.0,
            also_cast=None, bm=1024, bn=1024):
    m, k = a.shape
    bm = min(bm, m)
    assert ncols % bn == 0 and m % bm == 0 and scaled_cols % bn == 0
    n_panels = ncols // bn
    grid = (n_panels, m // bm)
    in_specs = [pl.BlockSpec((bm, k), lambda j, i: (i, 0)), pl.BlockSpec(memory_space=pl.ANY)]
    out_specs = [pl.BlockSpec((bm, bn), lambda j, i: (i, j))]
    out_shape = [jax.ShapeDtypeStruct((m, ncols), out_dtype)]
    scratch = [pltpu.VMEM((k, bn), F32), pltpu.VMEM((k, bn), BF16), pltpu.SemaphoreType.DMA(())]
    args = [a, w]
    cast_rows = 0
    if also_cast is not None:
        rows, cols = also_cast.shape
        cast_rows = rows // (grid[0] * grid[1])
        assert cast_rows * grid[0] * grid[1] == rows
        in_specs.append(pl.BlockSpec(memory_space=pl.ANY))
        out_specs.append(pl.BlockSpec(memory_space=pl.ANY))
        out_shape.append(jax.ShapeDtypeStruct((rows, cols), BF16))
        scratch += [pltpu.VMEM((cast_rows, cols), F32), pltpu.VMEM((cast_rows, cols), BF16),
                    pltpu.SemaphoreType.DMA((2,))]
        args.append(also_cast)
    out = pl.pallas_call(
        functools.partial(_matmul_kernel, gate=gate, col0=col0, bn=bn, n_panels=n_panels,
                          scaled_cols=scaled_cols, col_scale=col_scale, cast_rows=cast_rows),
        grid=grid,
        in_specs=in_specs,
        out_specs=out_specs,
        out_shape=out_shape,
        scratch_shapes=scratch,
        compiler_params=_params("arbitrary", "arbitrary"),
        name="matmul",
    )(*args)
    return out if also_cast is not None else out[0]


KV_BN = 512


def _kv_proj_kernel(a_ref, w_hbm, o_ref, stage_a, stage_b, wb_a, wb_b, sems, *, n_pairs):
    t = pl.program_id(0)
    last = t == n_pairs - 1

    def copy_a(p):
        return pltpu.make_async_copy(w_hbm.at[:, pl.ds(p * KV_BN, KV_BN)], stage_a, sems.at[0])

    def copy_b(p):
        return pltpu.make_async_copy(w_hbm.at[:, pl.ds(p * KV_BN, KV_BN)], stage_b, sems.at[1])

    @pl.when(t == 0)
    def _():
        copy_a(0).start()
        copy_b(1).start()
        copy_a(0).wait()
        wb_a[...] = stage_a[...].astype(BF16)

        if n_pairs > 1:
            copy_a(2).start()

    copy_b(2 * t + 1).wait()
    wb_b[...] = stage_b[...].astype(BF16)
    o_ref[:, 0:KV_BN] = jnp.dot(a_ref[...], wb_a[...],
                                preferred_element_type=F32).astype(o_ref.dtype)

    @pl.when(jnp.logical_not(last))
    def _():
        copy_b(2 * t + 3).start()
        copy_a(2 * t + 2).wait()
        o_ref[:, KV_BN:] = jnp.dot(a_ref[...], wb_b[...],
                                   preferred_element_type=F32).astype(o_ref.dtype)
        wb_a[...] = stage_a[...].astype(BF16)

        @pl.when(t + 2 < n_pairs)
        def _():
            copy_a(2 * t + 4).start()

    @pl.when(last)
    def _():
        o_ref[:, KV_BN:] = jnp.dot(a_ref[...], wb_b[...],
                                   preferred_element_type=F32).astype(o_ref.dtype)


def _kv_proj(a, w):
    m, k = a.shape
    n = w.shape[1]
    n_pairs = n // (2 * KV_BN)
    assert n_pairs * 2 * KV_BN == n
    return pl.pallas_call(
        functools.partial(_kv_proj_kernel, n_pairs=n_pairs),
        grid=(n_pairs,),
        in_specs=[pl.BlockSpec((m, k), lambda t: (0, 0)),
                  pl.BlockSpec(memory_space=pl.ANY)],
        out_specs=pl.BlockSpec((m, 2 * KV_BN), lambda t: (0, t)),
        out_shape=jax.ShapeDtypeStruct((m, n), BF16),
        scratch_shapes=[pltpu.VMEM((k, KV_BN), F32), pltpu.VMEM((k, KV_BN), F32),
                        pltpu.VMEM((k, KV_BN), BF16), pltpu.VMEM((k, KV_BN), BF16),
                        pltpu.SemaphoreType.DMA((2,))],
        compiler_params=_params("arbitrary"),
        name="kv_proj",
    )(a, w)


def _out_proj_kernel(ymix_ref, ymem_ref, w_ref, x_ref, o_ref):
    o_ref[...] = (x_ref[...]
                  + jnp.dot(ymix_ref[...], w_ref[0:D_MIX, :], preferred_element_type=F32)
                  + jnp.dot(ymem_ref[...], w_ref[D_MIX:, :], preferred_element_type=F32))


def _out_proj(ymix, ymem, w, x, *, bm=1024, bn=512):
    m = x.shape[0]
    k, n = w.shape
    return pl.pallas_call(
        _out_proj_kernel,
        grid=(m // bm, n // bn),
        in_specs=[
            pl.BlockSpec((bm, D_MIX), lambda i, j: (i, 0)),
            pl.BlockSpec((bm, D_MEM), lambda i, j: (i, 0)),
            pl.BlockSpec((k, bn), lambda i, j: (0, j)),
            pl.BlockSpec((bm, bn), lambda i, j: (i, j)),
        ],
        out_specs=pl.BlockSpec((bm, bn), lambda i, j: (i, j)),
        out_shape=jax.ShapeDtypeStruct((m, n), F32),
        compiler_params=_params("parallel", "arbitrary"),
        name="out_proj",
    )(ymix, ymem, w, x)


def _pool_kernel(u_ref, prev_ref, gate_ref, w_ref, scale_ref, o_ref, wb_ref, *, ts, tr):
    g = pl.program_id(0)
    si = pl.program_id(2)

    @pl.when(jnp.logical_and(pl.program_id(1) == 0, si == 0))
    def _():
        wb_ref[...] = w_ref[0].astype(BF16)

    for idx, window in enumerate(POOL_WINDOWS):
        @pl.when(g == idx)
        def _(window=window):
            _pool_tile(u_ref, prev_ref, gate_ref, scale_ref, o_ref, wb_ref, si,
                       ts=ts, tr=tr, window=window)


def _pool_tile(u_ref, prev_ref, gate_ref, scale_ref, o_ref, wb_ref, si, *, ts, tr, window):
    def pooled(rc):
        r0 = rc * tr
        if rc == 0:
            prev = jnp.where(si > 0, prev_ref[...].astype(F32), 0.0)
            xs = jnp.concatenate([prev, u_ref[0:tr, :].astype(F32)], axis=0)
        else:
            xs = u_ref[r0 - POOL_HALO:r0 + tr, :].astype(F32)
        s = xs
        k = 1
        while k < window:
            s = s + pltpu.roll(s, k, 0)
            k *= 2
        t = si * ts + r0 + lax.broadcasted_iota(jnp.int32, (tr, 1), 0)
        inv_cnt = 1.0 / jnp.minimum(t + 1, window).astype(F32)
        return (s[POOL_HALO:] * inv_cnt - xs[POOL_HALO:]).astype(BF16)

    n_sub = ts // tr
    nxt = pooled(0)
    for rc in range(n_sub):
        cur = nxt
        if rc + 1 < n_sub:
            nxt = pooled(rc + 1)
        rows = pl.ds(rc * tr, tr)
        mixed = jnp.dot(cur, wb_ref[...], preferred_element_type=F32)
        o_ref[rows, :] = (mixed * scale_ref[...] * gate_ref[rows, :].astype(F32)
                          ).astype(o_ref.dtype)


def _pool_mixer(u, gate, pool_w, pool_scale, *, batch, seq, ts=1024, tr=256):
    t_total = u.shape[0]
    c = POOL_GROUP
    n_s = seq // ts
    halo_blocks_per_tile = ts // POOL_HALO

    def prev_map(g, b, si):
        first = (b * n_s + si) * halo_blocks_per_tile
        return (jnp.maximum(first - 1, 0), g)

    tile = pl.BlockSpec((ts, c), lambda g, b, si: (b * n_s + si, g))
    return pl.pallas_call(
        functools.partial(_pool_kernel, ts=ts, tr=tr),
        grid=(len(POOL_WINDOWS), batch, n_s),
        in_specs=[tile, pl.BlockSpec((POOL_HALO, c), prev_map), tile,
                  pl.BlockSpec((1, c, c), lambda g, b, si: (g, 0, 0)),
                  pl.BlockSpec((1, c), lambda g, b, si: (0, g))],
        out_specs=tile,
        out_shape=jax.ShapeDtypeStruct((t_total, D_MIX), BF16),
        scratch_shapes=[pltpu.VMEM((c, c), BF16)],
        compiler_params=_params("arbitrary", "arbitrary", "arbitrary"),
        name="pool_mixer",
    )(u, u, gate, pool_w, pool_scale.reshape(1, D_MIX).astype(F32))


def _diff_attn_kernel(lq1_ref, lk1_ref, lq2_ref, lk2_ref, slope_ref, pos_ref,
                      q_ref, k_ref, v_ref, gate_ref, g_ref, o_ref, kaug_ref,
                      *, tq, tr, n_q, lam_init):
    qi = pl.program_id(2)

    f = lambda ref: ref[...].astype(F32)
    lam = (jnp.exp(jnp.sum(f(lq1_ref) * f(lk1_ref), keepdims=True))
           - jnp.exp(jnp.sum(f(lq2_ref) * f(lk2_ref), keepdims=True)) + lam_init)

    @pl.when(qi == 0)
    def _():
        for j in range(2):
            kaug_ref[j, :, 0:DIFF_QK] = k_ref[:, j * DIFF_QK:(j + 1) * DIFF_QK]
            kaug_ref[j, :, DIFF_QK:] = pos_ref[...]

    slope_cols = jnp.broadcast_to(slope_ref[0], (tr, DIFF_QK))

    def logits(j, r0, n_keys):
        sl = slice(j * DIFF_QK, (j + 1) * DIFF_QK)
        q_aug = jnp.concatenate([q_ref[pl.ds(r0, tr), sl], slope_cols], axis=-1)
        return lax.dot_general(q_aug, kaug_ref[j, 0:n_keys, :],
                               (((1,), (1,)), ((), ())), preferred_element_type=F32)

    def softmax(s, n_keys):
        n_past = n_keys - tr
        causal = (lax.broadcasted_iota(jnp.int32, (1, tr), 1)
                  <= lax.broadcasted_iota(jnp.int32, (tr, 1), 0))
        own = jnp.where(causal, s[:, n_past:], MASK_VALUE)
        m = jnp.max(own, axis=-1, keepdims=True)
        if n_past:
            past = s[:, :n_past]
            m = jnp.maximum(m, jnp.max(past, axis=-1, keepdims=True))
        p = jnp.exp(own - m)
        if n_past:
            p = jnp.concatenate([jnp.exp(past - m), p], axis=-1)
        return p.astype(BF16), jnp.sum(p, axis=-1, keepdims=True)

    subln_gain = g_ref[...] * (1.0 - lam_init)

    def finish(r0, pv, l):
        o = pv[0] * (1.0 / l[0]) - pv[1] * (lam / l[1])
        ms = jnp.mean(o * o, axis=-1, keepdims=True)
        y = o * lax.rsqrt(ms + SUBLN_EPS) * subln_gain
        gate = gate_ref[pl.ds(r0, tr), :].astype(F32)
        o_ref[pl.ds(r0, tr), :] = (y * gate).astype(o_ref.dtype)

    for c in range(n_q):
        @pl.when(qi == c)
        def _(c=c):
            chains = [(j, r * tr, c * tq + (r + 1) * tr)
                      for r in range(tq // tr) for j in range(2)]
            chains = chains[::-1]
            s_next = logits(*chains[0])
            pv, ls = {}, {}
            for n, (j, r0, n_keys) in enumerate(chains):
                s = s_next
                if n + 1 < len(chains):
                    s_next = logits(*chains[n + 1])
                p, ls[j] = softmax(s, n_keys)
                pv[j] = jnp.dot(p, v_ref[0:n_keys, :], preferred_element_type=F32)
                if len(pv) == 2:
                    finish(r0, pv, ls)
                    pv, ls = {}, {}


def _diff_attention(qkv, gate, lam_vecs, subln_g, *, batch, seq, lam_init, tq=2048, tr=256):
    t_total = qkv.shape[0]
    n_q = seq // tq
    slope_cols, pos_cols = _alibi_tables(seq)
    vec_spec = pl.BlockSpec((1, DIFF_QK), lambda b, h, qi: (0, 0))
    lam_vecs = [v.reshape(1, DIFF_QK).astype(F32) for v in lam_vecs]
    return pl.pallas_call(
        functools.partial(_diff_attn_kernel, tq=tq, tr=tr, n_q=n_q, lam_init=lam_init),
        grid=(batch, DIFF_HEADS, n_q),
        in_specs=[
            vec_spec, vec_spec, vec_spec, vec_spec,
            pl.BlockSpec((1, 1, DIFF_QK), lambda b, h, qi: (h, 0, 0)),
            pl.BlockSpec((seq, DIFF_QK), lambda b, h, qi: (0, 0)),
            pl.BlockSpec((tq, DIFF_V), lambda b, h, qi: (b * n_q + qi, h)),
            pl.BlockSpec((seq, DIFF_V), lambda b, h, qi: (b, DIFF_HEADS + h)),
            pl.BlockSpec((seq, DIFF_V), lambda b, h, qi: (b, 2 * DIFF_HEADS + h)),
            pl.BlockSpec((tq, DIFF_V), lambda b, h, qi: (b * n_q + qi, h)),
            pl.BlockSpec((1, DIFF_V), lambda b, h, qi: (0, 0)),
        ],
        out_specs=pl.BlockSpec((tq, DIFF_V), lambda b, h, qi: (b * n_q + qi, h)),
        out_shape=jax.ShapeDtypeStruct((t_total, D_MIX), BF16),
        scratch_shapes=[pltpu.VMEM((2, seq, 2 * DIFF_QK), BF16)],
        compiler_params=_params("parallel", "parallel", "arbitrary"),
        name="diff_attention",
    )(*lam_vecs, slope_cols, pos_cols, qkv, qkv, qkv, gate,
      subln_g.reshape(1, DIFF_V).astype(F32))


def _mem_attn_kernel(q_ref, k_ref, v_ref, gate_ref, o_ref, *, tr):
    scale = MEM_HEAD_DIM ** -0.5
    chains = [(r * tr, h) for r in range(q_ref.shape[0] // tr) for h in range(MEM_HEADS)]

    def logits(r0, h):
        sl = slice(h * MEM_HEAD_DIM, (h + 1) * MEM_HEAD_DIM)
        return lax.dot_general(q_ref[pl.ds(r0, tr), sl], k_ref[:, sl],
                               (((1,), (1,)), ((), ())), preferred_element_type=F32)

    s_next = logits(*chains[0])
    for n, (r0, h) in enumerate(chains):
        s = s_next * scale
        if n + 1 < len(chains):
            s_next = logits(*chains[n + 1])
        sl = slice(h * MEM_HEAD_DIM, (h + 1) * MEM_HEAD_DIM)
        m = jnp.max(s, axis=-1, keepdims=True)
        p = jnp.exp(s - m)
        l = jnp.sum(p, axis=-1, keepdims=True)
        o = jnp.dot(p.astype(BF16), v_ref[:, sl], preferred_element_type=F32) / l
        rows = pl.ds(r0, tr)
        o_ref[rows, sl] = (o * gate_ref[rows, sl].astype(F32)).astype(o_ref.dtype)


def _mem_attention(proj, q_col, kv, gate, *, batch, seq, mem_len, tq=1024, tr=512):
    t_total = proj.shape[0]
    n_q = seq // tq
    assert q_col % D_MEM == 0
    q_block = q_col // D_MEM
    z_block = D_MIX // D_MEM
    return pl.pallas_call(
        functools.partial(_mem_attn_kernel, tr=tr),
        grid=(batch, n_q),
        in_specs=[
            pl.BlockSpec((tq, D_MEM), lambda b, qi: (b * n_q + qi, q_block)),
            pl.BlockSpec((mem_len, D_MEM), lambda b, qi: (b, 0)),
            pl.BlockSpec((mem_len, D_MEM), lambda b, qi: (b, 1)),
            pl.BlockSpec((tq, D_MEM), lambda b, qi: (b * n_q + qi, z_block)),
        ],
        out_specs=pl.BlockSpec((tq, D_MEM), lambda b, qi: (b * n_q + qi, 0)),
        out_shape=jax.ShapeDtypeStruct((t_total, D_MEM), BF16),
        compiler_params=_params("parallel", "arbitrary"),
        name="mem_attention",
    )(proj, kv, kv, gate)


def _lambda_init(layer_idx):
    return 0.8 - 0.6 * math.exp(-0.3 * layer_idx)


def kernel(x, mem, l0_norm_g, l0_w_in, l0_pool_w, l0_pool_scale, l0_mem_norm_g,
           l0_w_mem_kv, l0_w_out, l1_norm_g, l1_w_in, l1_lambda_q1, l1_lambda_k1,
           l1_lambda_q2, l1_lambda_k2, l1_subln_g, l1_mem_norm_g, l1_w_mem_kv,
           l1_w_out, final_norm_g):
    batch, seq, d = x.shape
    mem_len = mem.shape[1]
    x2 = x.reshape(batch * seq, d)
    mem2 = mem.reshape(batch * mem_len, d)
    dims = dict(batch=batch, seq=seq)

    h = _rmsnorm(x2, l0_norm_g, BF16)
    proj = _matmul(h, l0_w_in, 0, D_MIX + D_MEM)
    gate, w_out = _matmul(h, l0_w_in, D_MIX + D_MEM, D_INNER, gate=True, also_cast=l0_w_out)
    kv = _kv_proj(_rmsnorm(mem2, l0_mem_norm_g, BF16), l0_w_mem_kv)
    y_mix = _pool_mixer(proj, gate, l0_pool_w, l0_pool_scale, **dims)
    y_mem = _mem_attention(proj, D_MIX, kv, gate, mem_len=mem_len, **dims)
    x2 = _out_proj(y_mix, y_mem, w_out, x2)

    h = _rmsnorm(x2, l1_norm_g, BF16)
    proj = _matmul(h, l1_w_in, 0, 3 * D_MIX + D_MEM, scaled_cols=D_MIX,
                   col_scale=DIFF_QK ** -0.5)
    gate, w_out = _matmul(h, l1_w_in, 3 * D_MIX + D_MEM, D_INNER, gate=True,
                          also_cast=l1_w_out)
    kv = _kv_proj(_rmsnorm(mem2, l1_mem_norm_g, BF16), l1_w_mem_kv)
    y_mix = _diff_attention(proj, gate, (l1_lambda_q1, l1_lambda_k1, l1_lambda_q2, l1_lambda_k2),
                            l1_subln_g, lam_init=_lambda_init(1), **dims)
    y_mem = _mem_attention(proj, 3 * D_MIX, kv, gate, mem_len=mem_len, **dims)
    x2 = _out_proj(y_mix, y_mem, w_out, x2)

    return _rmsnorm(x2, final_norm_g, F32).reshape(batch, seq, d)
```

```python
import functools
import math

import numpy as np
import jax
import jax.numpy as jnp
from jax import lax
from jax.experimental import pallas as pl
from jax.experimental.pallas import tpu as pltpu

F32 = jnp.float32
BF16 = jnp.bfloat16

D_MODEL = 4096
D_INNER = 2 * D_MODEL
D_MEM = D_INNER // 4
D_MIX = D_INNER - D_MEM
MEM_HEADS = 4
MEM_HEAD_DIM = D_MEM // MEM_HEADS
POOL_WINDOWS = (2, 4, 8, 16)
POOL_GROUP = D_MIX // len(POOL_WINDOWS)
POOL_HALO = 16
DIFF_QK = 128
DIFF_V = 2 * DIFF_QK
DIFF_HEADS = D_MIX // DIFF_V
RMS_EPS = 1e-6
SUBLN_EPS = 1e-5
MASK_VALUE = -1e30

V7X_VMEM_BYTES = 64 * 1024 * 1024
VMEM_LIMIT_BYTES = V7X_VMEM_BYTES - 4 * 1024 * 1024


def _params(*semantics):
    return pltpu.CompilerParams(dimension_semantics=semantics,
                                vmem_limit_bytes=VMEM_LIMIT_BYTES)


def _silu(z):
    half = 0.5 * z
    return half + half * jnp.tanh(half)


def _alibi_slopes(n):
    def pow2(m):
        start = 2.0 ** (-8.0 / m)
        return [start ** (i + 1) for i in range(m)]
    if math.log2(n).is_integer():
        s = pow2(n)
    else:
        c = 2 ** math.floor(math.log2(n))
        s = pow2(c) + pow2(2 * c)[0::2][: n - c]
    return np.asarray(s, dtype=np.float32)


POS_RADIX = 64
SLOPE_PIECES = 3


def _alibi_tables(seq):
    rest = _alibi_slopes(DIFF_HEADS).astype(np.float64)
    pieces = []
    for _ in range(SLOPE_PIECES):
        piece = rest.astype(BF16).astype(np.float64)
        pieces.append(piece)
        rest = rest - piece
    assert not rest.any(), "slope pieces must be exact"
    n = SLOPE_PIECES
    slope_cols = np.zeros((DIFF_HEADS, 1, DIFF_QK), np.float64)
    slope_cols[:, 0, 0:n] = POS_RADIX * np.stack(pieces, axis=-1)
    slope_cols[:, 0, n:2 * n] = np.stack(pieces, axis=-1)
    pos = np.arange(seq)
    pos_cols = np.zeros((seq, DIFF_QK), np.float64)
    pos_cols[:, 0:n] = (pos // POS_RADIX)[:, None]
    pos_cols[:, n:2 * n] = (pos % POS_RADIX)[:, None]
    return jnp.asarray(slope_cols, BF16), jnp.asarray(pos_cols, BF16)


def _rmsnorm_kernel(x_ref, g_ref, o_ref, *, eps):
    x = x_ref[...].astype(F32)
    ms = jnp.mean(x * x, axis=-1, keepdims=True)
    o_ref[...] = (x * lax.rsqrt(ms + eps) * g_ref[...]).astype(o_ref.dtype)


def _rmsnorm(x, g, out_dtype, *, eps=RMS_EPS):
    m, d = x.shape
    tm = min(512, m // 2)
    return pl.pallas_call(
        functools.partial(_rmsnorm_kernel, eps=eps),
        grid=(m // tm,),
        in_specs=[pl.BlockSpec((tm, d), lambda i: (i, 0)),
                  pl.BlockSpec((1, d), lambda i: (0, 0))],
        out_specs=pl.BlockSpec((tm, d), lambda i: (i, 0)),
        out_shape=jax.ShapeDtypeStruct((m, d), out_dtype),
        compiler_params=_params("parallel"),
        name="rmsnorm",
    )(x, g.reshape(1, d).astype(F32))


def _matmul_kernel(*refs, gate, col0, bn, n_panels, scaled_cols, col_scale, cast_rows):
    if cast_rows:
        (a_ref, w_hbm, src_hbm, o_ref, dst_hbm, stage_ref, wb_ref, sem,
         cast_in, cast_out, cast_sems) = refs
    else:
        a_ref, w_hbm, o_ref, stage_ref, wb_ref, sem = refs
    j = pl.program_id(0)

    if cast_rows:
        step = j * pl.num_programs(1) + pl.program_id(1)
        n_steps = n_panels * pl.num_programs(1)

        def fetch(t):
            return pltpu.make_async_copy(src_hbm.at[pl.ds(t * cast_rows, cast_rows), :],
                                         cast_in, cast_sems.at[0])

        def put(t):
            return pltpu.make_async_copy(cast_out,
                                         dst_hbm.at[pl.ds(t * cast_rows, cast_rows), :],
                                         cast_sems.at[1])

        @pl.when(step == 0)
        def _():
            fetch(0).start()

        fetch(step).wait()

        @pl.when(step > 0)
        def _():
            put(step - 1).wait()

        cast_out[...] = cast_in[...].astype(BF16)
        put(step).start()

        @pl.when(step + 1 < n_steps)
        def _():
            fetch(step + 1).start()

    def panel_copy(jj):
        return pltpu.make_async_copy(w_hbm.at[:, pl.ds(col0 + jj * bn, bn)], stage_ref, sem)

    @pl.when(pl.program_id(1) == 0)
    def _():
        @pl.when(j == 0)
        def _():
            panel_copy(0).start()

        panel_copy(j).wait()
        wb_ref[...] = stage_ref[...].astype(BF16)

        @pl.when(j + 1 < n_panels)
        def _():
            panel_copy(j + 1).start()

    acc = jnp.dot(a_ref[...], wb_ref[...], preferred_element_type=F32)
    if gate:
        acc = _silu(acc)
    if scaled_cols:
        acc = acc * jnp.where(j < scaled_cols // bn, col_scale, 1.0)
    o_ref[...] = acc.astype(o_ref.dtype)

    if cast_rows:
        @pl.when(step == n_steps - 1)
        def _():
            put(step).wait()


def _matmul(a, w, col0, ncols, out_dtype=BF16, *, gate=False, scaled_cols=0, col_scale=1.0,
            also_cast=None, bm=1024, bn=1024):
    m, k = a.shape
    bm = min(bm, m)
    assert ncols % bn == 0 and m % bm == 0 and scaled_cols % bn == 0
    n_panels = ncols // bn
    grid = (n_panels, m // bm)
    in_specs = [pl.BlockSpec((bm, k), lambda j, i: (i, 0)), pl.BlockSpec(memory_space=pl.ANY)]
    out_specs = [pl.BlockSpec((bm, bn), lambda j, i: (i, j))]
    out_shape = [jax.ShapeDtypeStruct((m, ncols), out_dtype)]
    scratch = [pltpu.VMEM((k, bn), F32), pltpu.VMEM((k, bn), BF16), pltpu.SemaphoreType.DMA(())]
    args = [a, w]
    cast_rows = 0
    if also_cast is not None:
        rows, cols = also_cast.shape
        cast_rows = rows // (grid[0] * grid[1])
        assert cast_rows * grid[0] * grid[1] == rows
        in_specs.append(pl.BlockSpec(memory_space=pl.ANY))
        out_specs.append(pl.BlockSpec(memory_space=pl.ANY))
        out_shape.append(jax.ShapeDtypeStruct((rows, cols), BF16))
        scratch += [pltpu.VMEM((cast_rows, cols), F32), pltpu.VMEM((cast_rows, cols), BF16),
                    pltpu.SemaphoreType.DMA((2,))]
        args.append(also_cast)
    out = pl.pallas_call(
        functools.partial(_matmul_kernel, gate=gate, col0=col0, bn=bn, n_panels=n_panels,
                          scaled_cols=scaled_cols, col_scale=col_scale, cast_rows=cast_rows),
        grid=grid,
        in_specs=in_specs,
        out_specs=out_specs,
        out_shape=out_shape,
        scratch_shapes=scratch,
        compiler_params=_params("arbitrary", "arbitrary"),
        name="matmul",
    )(*args)
    return out if also_cast is not None else out[0]


KV_BN = 512


def _kv_proj_kernel(a_ref, w_hbm, o_ref, stage_a, stage_b, wb_a, wb_b, sems, *, n_pairs):
    t = pl.program_id(0)
    last = t == n_pairs - 1

    def copy_a(p):
        return pltpu.make_async_copy(w_hbm.at[:, pl.ds(p * KV_BN, KV_BN)], stage_a, sems.at[0])

    def copy_b(p):
        return pltpu.make_async_copy(w_hbm.at[:, pl.ds(p * KV_BN, KV_BN)], stage_b, sems.at[1])

    @pl.when(t == 0)
    def _():
        copy_a(0).start()
        copy_b(1).start()
        copy_a(0).wait()
        wb_a[...] = stage_a[...].astype(BF16)

        if n_pairs > 1:
            copy_a(2).start()

    copy_b(2 * t + 1).wait()
    wb_b[...] = stage_b[...].astype(BF16)
    o_ref[:, 0:KV_BN] = jnp.dot(a_ref[...], wb_a[...],
                                preferred_element_type=F32).astype(o_ref.dtype)

    @pl.when(jnp.logical_not(last))
    def _():
        copy_b(2 * t + 3).start()
        copy_a(2 * t + 2).wait()
        o_ref[:, KV_BN:] = jnp.dot(a_ref[...], wb_b[...],
                                   preferred_element_type=F32).astype(o_ref.dtype)
        wb_a[...] = stage_a[...].astype(BF16)

        @pl.when(t + 2 < n_pairs)
        def _():
            copy_a(2 * t + 4).start()

    @pl.when(last)
    def _():
        o_ref[:, KV_BN:] = jnp.dot(a_ref[...], wb_b[...],
                                   preferred_element_type=F32).astype(o_ref.dtype)


def _kv_proj(a, w):
    m, k = a.shape
    n = w.shape[1]
    n_pairs = n // (2 * KV_BN)
    assert n_pairs * 2 * KV_BN == n
    return pl.pallas_call(
        functools.partial(_kv_proj_kernel, n_pairs=n_pairs),
        grid=(n_pairs,),
        in_specs=[pl.BlockSpec((m, k), lambda t: (0, 0)),
                  pl.BlockSpec(memory_space=pl.ANY)],
        out_specs=pl.BlockSpec((m, 2 * KV_BN), lambda t: (0, t)),
        out_shape=jax.ShapeDtypeStruct((m, n), BF16),
        scratch_shapes=[pltpu.VMEM((k, KV_BN), F32), pltpu.VMEM((k, KV_BN), F32),
                        pltpu.VMEM((k, KV_BN), BF16), pltpu.VMEM((k, KV_BN), BF16),
                        pltpu.SemaphoreType.DMA((2,))],
        compiler_params=_params("arbitrary"),
        name="kv_proj",
    )(a, w)


def _out_proj_kernel(ymix_ref, ymem_ref, w_ref, x_ref, o_ref):
    o_ref[...] = (x_ref[...]
                  + jnp.dot(ymix_ref[...], w_ref[0:D_MIX, :], preferred_element_type=F32)
                  + jnp.dot(ymem_ref[...], w_ref[D_MIX:, :], preferred_element_type=F32))


def _out_proj(ymix, ymem, w, x, *, bm=1024, bn=512):
    m = x.shape[0]
    k, n = w.shape
    return pl.pallas_call(
        _out_proj_kernel,
        grid=(m // bm, n // bn),
        in_specs=[
            pl.BlockSpec((bm, D_MIX), lambda i, j: (i, 0)),
            pl.BlockSpec((bm, D_MEM), lambda i, j: (i, 0)),
            pl.BlockSpec((k, bn), lambda i, j: (0, j)),
            pl.BlockSpec((bm, bn), lambda i, j: (i, j)),
        ],
        out_specs=pl.BlockSpec((bm, bn), lambda i, j: (i, j)),
        out_shape=jax.ShapeDtypeStruct((m, n), F32),
        compiler_params=_params("parallel", "arbitrary"),
        name="out_proj",
    )(ymix, ymem, w, x)


def _out_proj_norm_kernel(ymix_ref, ymem_ref, w_ref, x_ref, g_ref, o_ref, *, bn, n_blocks):
    j = pl.program_id(1)
    y = (x_ref[...]
         + jnp.dot(ymix_ref[...], w_ref[0:D_MIX, :], preferred_element_type=F32)
         + jnp.dot(ymem_ref[...], w_ref[D_MIX:, :], preferred_element_type=F32))
    o_ref[:, pl.ds(pl.multiple_of(j * bn, bn), bn)] = y

    @pl.when(j == n_blocks - 1)
    def _():
        v = o_ref[...]
        ms = jnp.mean(v * v, axis=-1, keepdims=True)
        o_ref[...] = v * lax.rsqrt(ms + RMS_EPS) * g_ref[...]


def _out_proj_norm(ymix, ymem, w, x, norm_g, *, bm=512, bn=512):
    m = x.shape[0]
    k, n = w.shape
    n_blocks = n // bn
    return pl.pallas_call(
        functools.partial(_out_proj_norm_kernel, bn=bn, n_blocks=n_blocks),
        grid=(m // bm, n_blocks),
        in_specs=[
            pl.BlockSpec((bm, D_MIX), lambda i, j: (i, 0)),
            pl.BlockSpec((bm, D_MEM), lambda i, j: (i, 0)),
            pl.BlockSpec((k, bn), lambda i, j: (0, j)),
            pl.BlockSpec((bm, bn), lambda i, j: (i, j)),
            pl.BlockSpec((1, n), lambda i, j: (0, 0)),
        ],
        out_specs=pl.BlockSpec((bm, n), lambda i, j: (i, 0)),
        out_shape=jax.ShapeDtypeStruct((m, n), F32),
        compiler_params=_params("parallel", "arbitrary"),
        name="out_proj_norm",
    )(ymix, ymem, w, x, norm_g.reshape(1, n).astype(F32))


def _pool_kernel(u_ref, prev_ref, gate_ref, w_ref, scale_ref, o_ref, wb_ref, *, ts, tr):
    g = pl.program_id(0)
    si = pl.program_id(2)

    @pl.when(jnp.logical_and(pl.program_id(1) == 0, si == 0))
    def _():
        wb_ref[...] = w_ref[0].astype(BF16)

    for idx, window in enumerate(POOL_WINDOWS):
        @pl.when(g == idx)
        def _(window=window):
            _pool_tile(u_ref, prev_ref, gate_ref, scale_ref, o_ref, wb_ref, si,
                       ts=ts, tr=tr, window=window)


def _pool_tile(u_ref, prev_ref, gate_ref, scale_ref, o_ref, wb_ref, si, *, ts, tr, window):
    def pooled(rc):
        r0 = rc * tr
        if rc == 0:
            prev = jnp.where(si > 0, prev_ref[...].astype(F32), 0.0)
            xs = jnp.concatenate([prev, u_ref[0:tr, :].astype(F32)], axis=0)
        else:
            xs = u_ref[r0 - POOL_HALO:r0 + tr, :].astype(F32)
        s = xs
        k = 1
        while k < window:
            s = s + pltpu.roll(s, k, 0)
            k *= 2
        t = si * ts + r0 + lax.broadcasted_iota(jnp.int32, (tr, 1), 0)
        inv_cnt = 1.0 / jnp.minimum(t + 1, window).astype(F32)
        return (s[POOL_HALO:] * inv_cnt - xs[POOL_HALO:]).astype(BF16)

    n_sub = ts // tr
    nxt = pooled(0)
    for rc in range(n_sub):
        cur = nxt
        if rc + 1 < n_sub:
            nxt = pooled(rc + 1)
        rows = pl.ds(rc * tr, tr)
        mixed = jnp.dot(cur, wb_ref[...], preferred_element_type=F32)
        o_ref[rows, :] = (mixed * scale_ref[...] * gate_ref[rows, :].astype(F32)
                          ).astype(o_ref.dtype)


def _pool_mixer(u, gate, pool_w, pool_scale, *, batch, seq, ts=1024, tr=256):
    t_total = u.shape[0]
    c = POOL_GROUP
    n_s = seq // ts
    halo_blocks_per_tile = ts // POOL_HALO

    def prev_map(g, b, si):
        first = (b * n_s + si) * halo_blocks_per_tile
        return (jnp.maximum(first - 1, 0), g)

    tile = pl.BlockSpec((ts, c), lambda g, b, si: (b * n_s + si, g))
    return pl.pallas_call(
        functools.partial(_pool_kernel, ts=ts, tr=tr),
        grid=(len(POOL_WINDOWS), batch, n_s),
        in_specs=[tile, pl.BlockSpec((POOL_HALO, c), prev_map), tile,
                  pl.BlockSpec((1, c, c), lambda g, b, si: (g, 0, 0)),
                  pl.BlockSpec((1, c), lambda g, b, si: (0, g))],
        out_specs=tile,
        out_shape=jax.ShapeDtypeStruct((t_total, D_MIX), BF16),
        scratch_shapes=[pltpu.VMEM((c, c), BF16)],
        compiler_params=_params("arbitrary", "arbitrary", "arbitrary"),
        name="pool_mixer",
    )(u, u, gate, pool_w, pool_scale.reshape(1, D_MIX).astype(F32))


def _diff_attn_kernel(lq1_ref, lk1_ref, lq2_ref, lk2_ref, slope_ref, pos_ref,
                      q_ref, k_ref, v_ref, gate_ref, g_ref, o_ref, kaug_ref,
                      *, tq, tr, n_q, lam_init):
    qi = pl.program_id(2)

    f = lambda ref: ref[...].astype(F32)
    lam = (jnp.exp(jnp.sum(f(lq1_ref) * f(lk1_ref), keepdims=True))
           - jnp.exp(jnp.sum(f(lq2_ref) * f(lk2_ref), keepdims=True)) + lam_init)

    @pl.when(qi == 0)
    def _():
        for j in range(2):
            kaug_ref[j, :, 0:DIFF_QK] = k_ref[:, j * DIFF_QK:(j + 1) * DIFF_QK]
            kaug_ref[j, :, DIFF_QK:] = pos_ref[...]

    slope_cols = jnp.broadcast_to(slope_ref[0], (tr, DIFF_QK))

    def logits(j, r0, n_keys):
        sl = slice(j * DIFF_QK, (j + 1) * DIFF_QK)
        q_aug = jnp.concatenate([q_ref[pl.ds(r0, tr), sl], slope_cols], axis=-1)
        return lax.dot_general(q_aug, kaug_ref[j, 0:n_keys, :],
                               (((1,), (1,)), ((), ())), preferred_element_type=F32)

    def softmax(s, n_keys):
        n_past = n_keys - tr
        causal = (lax.broadcasted_iota(jnp.int32, (1, tr), 1)
                  <= lax.broadcasted_iota(jnp.int32, (tr, 1), 0))
        own = jnp.where(causal, s[:, n_past:], MASK_VALUE)
        m = jnp.max(own, axis=-1, keepdims=True)
        if n_past:
            past = s[:, :n_past]
            m = jnp.maximum(m, jnp.max(past, axis=-1, keepdims=True))
        p = jnp.exp(own - m)
        if n_past:
            p = jnp.concatenate([jnp.exp(past - m), p], axis=-1)
        return p.astype(BF16), jnp.sum(p, axis=-1, keepdims=True)

    subln_gain = g_ref[...] * (1.0 - lam_init)

    def finish(r0, pv, l):
        o = pv[0] * (1.0 / l[0]) - pv[1] * (lam / l[1])
        ms = jnp.mean(o * o, axis=-1, keepdims=True)
        y = o * lax.rsqrt(ms + SUBLN_EPS) * subln_gain
        gate = gate_ref[pl.ds(r0, tr), :].astype(F32)
        o_ref[pl.ds(r0, tr), :] = (y * gate).astype(o_ref.dtype)

    for c in range(n_q):
        @pl.when(qi == c)
        def _(c=c):
            chains = [(j, r * tr, c * tq + (r + 1) * tr)
                      for r in range(tq // tr) for j in range(2)]
            chains = chains[::-1]
            s_next = logits(*chains[0])
            pv, ls = {}, {}
            for n, (j, r0, n_keys) in enumerate(chains):
                s = s_next
                if n + 1 < len(chains):
                    s_next = logits(*chains[n + 1])
                p, ls[j] = softmax(s, n_keys)
                pv[j] = jnp.dot(p, v_ref[0:n_keys, :], preferred_element_type=F32)
                if len(pv) == 2:
                    finish(r0, pv, ls)
                    pv, ls = {}, {}


def _diff_attention(qkv, gate, lam_vecs, subln_g, *, batch, seq, lam_init, tq=2048, tr=256):
    t_total = qkv.shape[0]
    n_q = seq // tq
    slope_cols, pos_cols = _alibi_tables(seq)
    vec_spec = pl.BlockSpec((1, DIFF_QK), lambda b, h, qi: (0, 0))
    lam_vecs = [v.reshape(1, DIFF_QK).astype(F32) for v in lam_vecs]
    return pl.pallas_call(
        functools.partial(_diff_attn_kernel, tq=tq, tr=tr, n_q=n_q, lam_init=lam_init),
        grid=(batch, DIFF_HEADS, n_q),
        in_specs=[
            vec_spec, vec_spec, vec_spec, vec_spec,
            pl.BlockSpec((1, 1, DIFF_QK), lambda b, h, qi: (h, 0, 0)),
            pl.BlockSpec((seq, DIFF_QK), lambda b, h, qi: (0, 0)),
            pl.BlockSpec((tq, DIFF_V), lambda b, h, qi: (b * n_q + qi, h)),
            pl.BlockSpec((seq, DIFF_V), lambda b, h, qi: (b, DIFF_HEADS + h)),
            pl.BlockSpec((seq, DIFF_V), lambda b, h, qi: (b, 2 * DIFF_HEADS + h)),
            pl.BlockSpec((tq, DIFF_V), lambda b, h, qi: (b * n_q + qi, h)),
            pl.BlockSpec((1, DIFF_V), lambda b, h, qi: (0, 0)),
        ],
        out_specs=pl.BlockSpec((tq, DIFF_V), lambda b, h, qi: (b * n_q + qi, h)),
        out_shape=jax.ShapeDtypeStruct((t_total, D_MIX), BF16),
        scratch_shapes=[pltpu.VMEM((2, seq, 2 * DIFF_QK), BF16)],
        compiler_params=_params("parallel", "parallel", "arbitrary"),
        name="diff_attention",
    )(*lam_vecs, slope_cols, pos_cols, qkv, qkv, qkv, gate,
      subln_g.reshape(1, DIFF_V).astype(F32))


def _mem_attn_kernel(q_ref, k_ref, v_ref, gate_ref, o_ref, *, tr):
    scale = MEM_HEAD_DIM ** -0.5
    chains = [(r * tr, h) for r in range(q_ref.shape[0] // tr) for h in range(MEM_HEADS)]

    def logits(r0, h):
        sl = slice(h * MEM_HEAD_DIM, (h + 1) * MEM_HEAD_DIM)
        return lax.dot_general(q_ref[pl.ds(r0, tr), sl], k_ref[:, sl],
                               (((1,), (1,)), ((), ())), preferred_element_type=F32)

    s_next = logits(*chains[0])
    for n, (r0, h) in enumerate(chains):
        s = s_next * scale
        if n + 1 < len(chains):
            s_next = logits(*chains[n + 1])
        sl = slice(h * MEM_HEAD_DIM, (h + 1) * MEM_HEAD_DIM)
        m = jnp.max(s, axis=-1, keepdims=True)
        p = jnp.exp(s - m)
        l = jnp.sum(p, axis=-1, keepdims=True)
        o = jnp.dot(p.astype(BF16), v_ref[:, sl], preferred_element_type=F32) / l
        rows = pl.ds(r0, tr)
        o_ref[rows, sl] = (o * gate_ref[rows, sl].astype(F32)).astype(o_ref.dtype)


def _mem_attention(proj, q_col, kv, gate, *, batch, seq, mem_len, tq=1024, tr=512):
    t_total = proj.shape[0]
    n_q = seq // tq
    assert q_col % D_MEM == 0
    q_block = q_col // D_MEM
    z_block = D_MIX // D_MEM
    return pl.pallas_call(
        functools.partial(_mem_attn_kernel, tr=tr),
        grid=(batch, n_q),
        in_specs=[
            pl.BlockSpec((tq, D_MEM), lambda b, qi: (b * n_q + qi, q_block)),
            pl.BlockSpec((mem_len, D_MEM), lambda b, qi: (b, 0)),
            pl.BlockSpec((mem_len, D_MEM), lambda b, qi: (b, 1)),
            pl.BlockSpec((tq, D_MEM), lambda b, qi: (b * n_q + qi, z_block)),
        ],
        out_specs=pl.BlockSpec((tq, D_MEM), lambda b, qi: (b * n_q + qi, 0)),
        out_shape=jax.ShapeDtypeStruct((t_total, D_MEM), BF16),
        compiler_params=_params("parallel", "arbitrary"),
        name="mem_attention",
    )(proj, kv, kv, gate)


def _lambda_init(layer_idx):
    return 0.8 - 0.6 * math.exp(-0.3 * layer_idx)


def kernel(x, mem, l0_norm_g, l0_w_in, l0_pool_w, l0_pool_scale, l0_mem_norm_g,
           l0_w_mem_kv, l0_w_out, l1_norm_g, l1_w_in, l1_lambda_q1, l1_lambda_k1,
           l1_lambda_q2, l1_lambda_k2, l1_subln_g, l1_mem_norm_g, l1_w_mem_kv,
           l1_w_out, final_norm_g):
    batch, seq, d = x.shape
    mem_len = mem.shape[1]
    x2 = x.reshape(batch * seq, d)
    mem2 = mem.reshape(batch * mem_len, d)
    dims = dict(batch=batch, seq=seq)

    h = _rmsnorm(x2, l0_norm_g, BF16)
    proj = _matmul(h, l0_w_in, 0, D_MIX + D_MEM)
    gate, w_out = _matmul(h, l0_w_in, D_MIX + D_MEM, D_INNER, gate=True, also_cast=l0_w_out)
    kv = _kv_proj(_rmsnorm(mem2, l0_mem_norm_g, BF16), l0_w_mem_kv)
    y_mix = _pool_mixer(proj, gate, l0_pool_w, l0_pool_scale, **dims)
    y_mem = _mem_attention(proj, D_MIX, kv, gate, mem_len=mem_len, **dims)
    x2 = _out_proj(y_mix, y_mem, w_out, x2)

    h = _rmsnorm(x2, l1_norm_g, BF16)
    proj = _matmul(h, l1_w_in, 0, 3 * D_MIX + D_MEM, scaled_cols=D_MIX,
                   col_scale=DIFF_QK ** -0.5)
    gate, w_out = _matmul(h, l1_w_in, 3 * D_MIX + D_MEM, D_INNER, gate=True,
                          also_cast=l1_w_out)
    kv = _kv_proj(_rmsnorm(mem2, l1_mem_norm_g, BF16), l1_w_mem_kv)
    y_mix = _diff_attention(proj, gate, (l1_lambda_q1, l1_lambda_k1, l1_lambda_q2, l1_lambda_k2),
                            l1_subln_g, lam_init=_lambda_init(1), **dims)
    y_mem = _mem_attention(proj, 3 * D_MIX, kv, gate, mem_len=mem_len, **dims)
    return _out_proj_norm(y_mix, y_mem, w_out, x2, final_norm_g).reshape(batch, seq, d)
```

```python
import functools
import math

import numpy as np
import jax
import jax.numpy as jnp
from jax import lax
from jax.experimental import pallas as pl
from jax.experimental.pallas import tpu as pltpu

F32 = jnp.float32
BF16 = jnp.bfloat16

D_MODEL = 4096
D_INNER = 2 * D_MODEL
D_MEM = D_INNER // 4
D_MIX = D_INNER - D_MEM
MEM_HEADS = 4
MEM_HEAD_DIM = D_MEM // MEM_HEADS
POOL_WINDOWS = (2, 4, 8, 16)
POOL_GROUP = D_MIX // len(POOL_WINDOWS)
POOL_HALO = 16
DIFF_QK = 128
DIFF_V = 2 * DIFF_QK
DIFF_HEADS = D_MIX // DIFF_V
RMS_EPS = 1e-6
SUBLN_EPS = 1e-5
MASK_VALUE = -1e30

V7X_VMEM_BYTES = 64 * 1024 * 1024
VMEM_LIMIT_BYTES = V7X_VMEM_BYTES - 4 * 1024 * 1024


def _params(*semantics):
    return pltpu.CompilerParams(dimension_semantics=semantics,
                                vmem_limit_bytes=VMEM_LIMIT_BYTES)


def _silu(z):
    half = 0.5 * z
    return half + half * jnp.tanh(half)


def _alibi_slopes(n):
    def pow2(m):
        start = 2.0 ** (-8.0 / m)
        return [start ** (i + 1) for i in range(m)]
    if math.log2(n).is_integer():
        s = pow2(n)
    else:
        c = 2 ** math.floor(math.log2(n))
        s = pow2(c) + pow2(2 * c)[0::2][: n - c]
    return np.asarray(s, dtype=np.float32)


POS_RADIX = 64
SLOPE_PIECES = 3


def _alibi_tables(seq):
    rest = _alibi_slopes(DIFF_HEADS).astype(np.float64)
    pieces = []
    for _ in range(SLOPE_PIECES):
        piece = rest.astype(BF16).astype(np.float64)
        pieces.append(piece)
        rest = rest - piece
    assert not rest.any(), "slope pieces must be exact"
    n = SLOPE_PIECES
    slope_cols = np.zeros((DIFF_HEADS, 1, DIFF_QK), np.float64)
    slope_cols[:, 0, 0:n] = POS_RADIX * np.stack(pieces, axis=-1)
    slope_cols[:, 0, n:2 * n] = np.stack(pieces, axis=-1)
    pos = np.arange(seq)
    pos_cols = np.zeros((seq, DIFF_QK), np.float64)
    pos_cols[:, 0:n] = (pos // POS_RADIX)[:, None]
    pos_cols[:, n:2 * n] = (pos % POS_RADIX)[:, None]
    return jnp.asarray(slope_cols, BF16), jnp.asarray(pos_cols, BF16)


def _rmsnorm_kernel(x_ref, g_ref, o_ref, *, eps):
    x = x_ref[...].astype(F32)
    ms = jnp.mean(x * x, axis=-1, keepdims=True)
    o_ref[...] = (x * lax.rsqrt(ms + eps) * g_ref[...]).astype(o_ref.dtype)


NORM_RING = 3


def _rmsnorm_ring_kernel(x_hbm, g_ref, o_ref, ring_ref, sems, *, eps, tm, n_steps):
    s = pl.program_id(0)

    def fetch(t):
        slot = t % NORM_RING
        return pltpu.make_async_copy(x_hbm.at[pl.ds(t * tm, tm), :], ring_ref.at[slot],
                                     sems.at[slot])

    @pl.when(s == 0)
    def _():
        for t in range(NORM_RING - 1):
            fetch(t).start()

    @pl.when(s + NORM_RING - 1 < n_steps)
    def _():
        fetch(s + NORM_RING - 1).start()

    fetch(s).wait()
    x = ring_ref[s % NORM_RING]
    ms = jnp.mean(x * x, axis=-1, keepdims=True)
    o_ref[...] = (x * lax.rsqrt(ms + eps) * g_ref[...]).astype(o_ref.dtype)


def _rmsnorm_ring(x, g, out_dtype, *, eps=RMS_EPS, tm=512):
    m, d = x.shape
    n_steps = m // tm
    assert n_steps >= NORM_RING
    return pl.pallas_call(
        functools.partial(_rmsnorm_ring_kernel, eps=eps, tm=tm, n_steps=n_steps),
        grid=(n_steps,),
        in_specs=[pl.BlockSpec(memory_space=pl.ANY),
                  pl.BlockSpec((1, d), lambda i: (0, 0))],
        out_specs=pl.BlockSpec((tm, d), lambda i: (i, 0)),
        out_shape=jax.ShapeDtypeStruct((m, d), out_dtype),
        scratch_shapes=[pltpu.VMEM((NORM_RING, tm, d), F32),
                        pltpu.SemaphoreType.DMA((NORM_RING,))],
        compiler_params=_params("arbitrary"),
        name="rmsnorm_ring",
    )(x, g.reshape(1, d).astype(F32))


def _rmsnorm(x, g, out_dtype, *, eps=RMS_EPS):
    m, d = x.shape
    tm = min(512, m // 2)
    return pl.pallas_call(
        functools.partial(_rmsnorm_kernel, eps=eps),
        grid=(m // tm,),
        in_specs=[pl.BlockSpec((tm, d), lambda i: (i, 0)),
                  pl.BlockSpec((1, d), lambda i: (0, 0))],
        out_specs=pl.BlockSpec((tm, d), lambda i: (i, 0)),
        out_shape=jax.ShapeDtypeStruct((m, d), out_dtype),
        compiler_params=_params("parallel"),
        name="rmsnorm",
    )(x, g.reshape(1, d).astype(F32))


def _matmul_kernel(*refs, gate, col0, bn, n_panels, scaled_cols, col_scale, cast_rows):
    if cast_rows:
        (a_ref, w_hbm, src_hbm, o_ref, dst_hbm, stage_ref, wb_ref, sem,
         cast_in, cast_out, cast_sems) = refs
    else:
        a_ref, w_hbm, o_ref, stage_ref, wb_ref, sem = refs
    j = pl.program_id(0)

    if cast_rows:
        step = j * pl.num_programs(1) + pl.program_id(1)
        n_steps = n_panels * pl.num_programs(1)

        def fetch(t):
            return pltpu.make_async_copy(src_hbm.at[pl.ds(t * cast_rows, cast_rows), :],
                                         cast_in, cast_sems.at[0])

        def put(t):
            return pltpu.make_async_copy(cast_out,
                                         dst_hbm.at[pl.ds(t * cast_rows, cast_rows), :],
                                         cast_sems.at[1])

        @pl.when(step == 0)
        def _():
            fetch(0).start()

        fetch(step).wait()

        @pl.when(step > 0)
        def _():
            put(step - 1).wait()

        cast_out[...] = cast_in[...].astype(BF16)
        put(step).start()

        @pl.when(step + 1 < n_steps)
        def _():
            fetch(step + 1).start()

    def panel_copy(jj):
        return pltpu.make_async_copy(w_hbm.at[:, pl.ds(col0 + jj * bn, bn)], stage_ref, sem)

    @pl.when(pl.program_id(1) == 0)
    def _():
        @pl.when(j == 0)
        def _():
            panel_copy(0).start()

        panel_copy(j).wait()
        wb_ref[...] = stage_ref[...].astype(BF16)

        @pl.when(j + 1 < n_panels)
        def _():
            panel_copy(j + 1).start()

    acc = jnp.dot(a_ref[...], wb_ref[...], preferred_element_type=F32)
    if gate:
        acc = _silu(acc)
    if scaled_cols:
        acc = acc * jnp.where(j < scaled_cols // bn, col_scale, 1.0)
    o_ref[...] = acc.astype(o_ref.dtype)

    if cast_rows:
        @pl.when(step == n_steps - 1)
        def _():
            put(step).wait()


def _matmul(a, w, col0, ncols, out_dtype=BF16, *, gate=False, scaled_cols=0, col_scale=1.0,
            also_cast=None, bm=1024, bn=1024):
    m, k = a.shape
    bm = min(bm, m)
    assert ncols % bn == 0 and m % bm == 0 and scaled_cols % bn == 0
    n_panels = ncols // bn
    grid = (n_panels, m // bm)
    in_specs = [pl.BlockSpec((bm, k), lambda j, i: (i, 0)), pl.BlockSpec(memory_space=pl.ANY)]
    out_specs = [pl.BlockSpec((bm, bn), lambda j, i: (i, j))]
    out_shape = [jax.ShapeDtypeStruct((m, ncols), out_dtype)]
    scratch = [pltpu.VMEM((k, bn), F32), pltpu.VMEM((k, bn), BF16), pltpu.SemaphoreType.DMA(())]
    args = [a, w]
    cast_rows = 0
    if also_cast is not None:
        rows, cols = also_cast.shape
        cast_rows = rows // (grid[0] * grid[1])
        assert cast_rows * grid[0] * grid[1] == rows
        in_specs.append(pl.BlockSpec(memory_space=pl.ANY))
        out_specs.append(pl.BlockSpec(memory_space=pl.ANY))
        out_shape.append(jax.ShapeDtypeStruct((rows, cols), BF16))
        scratch += [pltpu.VMEM((cast_rows, cols), F32), pltpu.VMEM((cast_rows, cols), BF16),
                    pltpu.SemaphoreType.DMA((2,))]
        args.append(also_cast)
    out = pl.pallas_call(
        functools.partial(_matmul_kernel, gate=gate, col0=col0, bn=bn, n_panels=n_panels,
                          scaled_cols=scaled_cols, col_scale=col_scale, cast_rows=cast_rows),
        grid=grid,
        in_specs=in_specs,
        out_specs=out_specs,
        out_shape=out_shape,
        scratch_shapes=scratch,
        compiler_params=_params("arbitrary", "arbitrary"),
        name="matmul",
    )(*args)
    return out if also_cast is not None else out[0]


KV_BN = 512


def _kv_proj_kernel(a_ref, w_hbm, o_ref, stage_a, stage_b, wb_a, wb_b, sems, *, n_pairs):
    t = pl.program_id(0)
    last = t == n_pairs - 1

    def copy_a(p):
        return pltpu.make_async_copy(w_hbm.at[:, pl.ds(p * KV_BN, KV_BN)], stage_a, sems.at[0])

    def copy_b(p):
        return pltpu.make_async_copy(w_hbm.at[:, pl.ds(p * KV_BN, KV_BN)], stage_b, sems.at[1])

    @pl.when(t == 0)
    def _():
        copy_a(0).start()
        copy_b(1).start()
        copy_a(0).wait()
        wb_a[...] = stage_a[...].astype(BF16)

        if n_pairs > 1:
            copy_a(2).start()

    copy_b(2 * t + 1).wait()
    wb_b[...] = stage_b[...].astype(BF16)
    o_ref[:, 0:KV_BN] = jnp.dot(a_ref[...], wb_a[...],
                                preferred_element_type=F32).astype(o_ref.dtype)

    @pl.when(jnp.logical_not(last))
    def _():
        copy_b(2 * t + 3).start()
        copy_a(2 * t + 2).wait()
        o_ref[:, KV_BN:] = jnp.dot(a_ref[...], wb_b[...],
                                   preferred_element_type=F32).astype(o_ref.dtype)
        wb_a[...] = stage_a[...].astype(BF16)

        @pl.when(t + 2 < n_pairs)
        def _():
            copy_a(2 * t + 4).start()

    @pl.when(last)
    def _():
        o_ref[:, KV_BN:] = jnp.dot(a_ref[...], wb_b[...],
                                   preferred_element_type=F32).astype(o_ref.dtype)


def _kv_proj(a, w):
    m, k = a.shape
    n = w.shape[1]
    n_pairs = n // (2 * KV_BN)
    assert n_pairs * 2 * KV_BN == n
    return pl.pallas_call(
        functools.partial(_kv_proj_kernel, n_pairs=n_pairs),
        grid=(n_pairs,),
        in_specs=[pl.BlockSpec((m, k), lambda t: (0, 0)),
                  pl.BlockSpec(memory_space=pl.ANY)],
        out_specs=pl.BlockSpec((m, 2 * KV_BN), lambda t: (0, t)),
        out_shape=jax.ShapeDtypeStruct((m, n), BF16),
        scratch_shapes=[pltpu.VMEM((k, KV_BN), F32), pltpu.VMEM((k, KV_BN), F32),
                        pltpu.VMEM((k, KV_BN), BF16), pltpu.VMEM((k, KV_BN), BF16),
                        pltpu.SemaphoreType.DMA((2,))],
        compiler_params=_params("arbitrary"),
        name="kv_proj",
    )(a, w)


def _out_proj_kernel(ymix_ref, ymem_ref, w_ref, x_ref, o_ref):
    o_ref[...] = (x_ref[...]
                  + jnp.dot(ymix_ref[...], w_ref[0:D_MIX, :], preferred_element_type=F32)
                  + jnp.dot(ymem_ref[...], w_ref[D_MIX:, :], preferred_element_type=F32))


def _out_proj(ymix, ymem, w, x, *, bm=1024, bn=512):
    m = x.shape[0]
    k, n = w.shape
    return pl.pallas_call(
        _out_proj_kernel,
        grid=(m // bm, n // bn),
        in_specs=[
            pl.BlockSpec((bm, D_MIX), lambda i, j: (i, 0)),
            pl.BlockSpec((bm, D_MEM), lambda i, j: (i, 0)),
            pl.BlockSpec((k, bn), lambda i, j: (0, j)),
            pl.BlockSpec((bm, bn), lambda i, j: (i, j)),
        ],
        out_specs=pl.BlockSpec((bm, bn), lambda i, j: (i, j)),
        out_shape=jax.ShapeDtypeStruct((m, n), F32),
        compiler_params=_params("parallel", "arbitrary"),
        name="out_proj",
    )(ymix, ymem, w, x)


def _out_proj_norm_kernel(ymix_ref, ymem_ref, w_ref, x_ref, g_ref, o_ref, *, bn, n_blocks):
    j = pl.program_id(1)
    y = (x_ref[...]
         + jnp.dot(ymix_ref[...], w_ref[0:D_MIX, :], preferred_element_type=F32)
         + jnp.dot(ymem_ref[...], w_ref[D_MIX:, :], preferred_element_type=F32))
    o_ref[:, pl.ds(pl.multiple_of(j * bn, bn), bn)] = y

    @pl.when(j == n_blocks - 1)
    def _():
        v = o_ref[...]
        ms = jnp.mean(v * v, axis=-1, keepdims=True)
        o_ref[...] = v * lax.rsqrt(ms + RMS_EPS) * g_ref[...]


def _out_proj_norm(ymix, ymem, w, x, norm_g, *, bm=512, bn=512):
    m = x.shape[0]
    k, n = w.shape
    n_blocks = n // bn
    return pl.pallas_call(
        functools.partial(_out_proj_norm_kernel, bn=bn, n_blocks=n_blocks),
        grid=(m // bm, n_blocks),
        in_specs=[
            pl.BlockSpec((bm, D_MIX), lambda i, j: (i, 0)),
            pl.BlockSpec((bm, D_MEM), lambda i, j: (i, 0)),
            pl.BlockSpec((k, bn), lambda i, j: (0, j)),
            pl.BlockSpec((bm, bn), lambda i, j: (i, j)),
            pl.BlockSpec((1, n), lambda i, j: (0, 0)),
        ],
        out_specs=pl.BlockSpec((bm, n), lambda i, j: (i, 0)),
        out_shape=jax.ShapeDtypeStruct((m, n), F32),
        compiler_params=_params("parallel", "arbitrary"),
        name="out_proj_norm",
    )(ymix, ymem, w, x, norm_g.reshape(1, n).astype(F32))


def _pool_kernel(u_ref, prev_ref, gate_ref, w_ref, scale_ref, o_ref, wb_ref, *, ts, tr):
    g = pl.program_id(0)
    si = pl.program_id(2)

    @pl.when(jnp.logical_and(pl.program_id(1) == 0, si == 0))
    def _():
        wb_ref[...] = w_ref[0].astype(BF16)

    for idx, window in enumerate(POOL_WINDOWS):
        @pl.when(g == idx)
        def _(window=window):
            _pool_tile(u_ref, prev_ref, gate_ref, scale_ref, o_ref, wb_ref, si,
                       ts=ts, tr=tr, window=window)


def _pool_tile(u_ref, prev_ref, gate_ref, scale_ref, o_ref, wb_ref, si, *, ts, tr, window):
    def pooled(rc):
        r0 = rc * tr
        if rc == 0:
            prev = jnp.where(si > 0, prev_ref[...].astype(F32), 0.0)
            xs = jnp.concatenate([prev, u_ref[0:tr, :].astype(F32)], axis=0)
        else:
            xs = u_ref[r0 - POOL_HALO:r0 + tr, :].astype(F32)
        s = xs
        k = 1
        while k < window:
            s = s + pltpu.roll(s, k, 0)
            k *= 2
        t = si * ts + r0 + lax.broadcasted_iota(jnp.int32, (tr, 1), 0)
        inv_cnt = 1.0 / jnp.minimum(t + 1, window).astype(F32)
        return (s[POOL_HALO:] * inv_cnt - xs[POOL_HALO:]).astype(BF16)

    n_sub = ts // tr
    nxt = pooled(0)
    for rc in range(n_sub):
        cur = nxt
        if rc + 1 < n_sub:
            nxt = pooled(rc + 1)
        rows = pl.ds(rc * tr, tr)
        mixed = jnp.dot(cur, wb_ref[...], preferred_element_type=F32)
        o_ref[rows, :] = (mixed * scale_ref[...] * gate_ref[rows, :].astype(F32)
                          ).astype(o_ref.dtype)


def _pool_mixer(u, gate, pool_w, pool_scale, *, batch, seq, ts=1024, tr=256):
    t_total = u.shape[0]
    c = POOL_GROUP
    n_s = seq // ts
    halo_blocks_per_tile = ts // POOL_HALO

    def prev_map(g, b, si):
        first = (b * n_s + si) * halo_blocks_per_tile
        return (jnp.maximum(first - 1, 0), g)

    tile = pl.BlockSpec((ts, c), lambda g, b, si: (b * n_s + si, g))
    return pl.pallas_call(
        functools.partial(_pool_kernel, ts=ts, tr=tr),
        grid=(len(POOL_WINDOWS), batch, n_s),
        in_specs=[tile, pl.BlockSpec((POOL_HALO, c), prev_map), tile,
                  pl.BlockSpec((1, c, c), lambda g, b, si: (g, 0, 0)),
                  pl.BlockSpec((1, c), lambda g, b, si: (0, g))],
        out_specs=tile,
        out_shape=jax.ShapeDtypeStruct((t_total, D_MIX), BF16),
        scratch_shapes=[pltpu.VMEM((c, c), BF16)],
        compiler_params=_params("arbitrary", "arbitrary", "arbitrary"),
        name="pool_mixer",
    )(u, u, gate, pool_w, pool_scale.reshape(1, D_MIX).astype(F32))


def _diff_attn_kernel(lq1_ref, lk1_ref, lq2_ref, lk2_ref, slope_ref, pos_ref,
                      q_ref, k_ref, v_ref, gate_ref, g_ref, o_ref, kaug_ref,
                      *, tq, tr, n_q, lam_init):
    qi = pl.program_id(2)

    f = lambda ref: ref[...].astype(F32)
    lam = (jnp.exp(jnp.sum(f(lq1_ref) * f(lk1_ref), keepdims=True))
           - jnp.exp(jnp.sum(f(lq2_ref) * f(lk2_ref), keepdims=True)) + lam_init)

    @pl.when(qi == 0)
    def _():
        for j in range(2):
            kaug_ref[j, :, 0:DIFF_QK] = k_ref[:, j * DIFF_QK:(j + 1) * DIFF_QK]
            kaug_ref[j, :, DIFF_QK:] = pos_ref[...]

    slope_cols = jnp.broadcast_to(slope_ref[0], (tr, DIFF_QK))

    def logits(j, r0, n_keys):
        sl = slice(j * DIFF_QK, (j + 1) * DIFF_QK)
        q_aug = jnp.concatenate([q_ref[pl.ds(r0, tr), sl], slope_cols], axis=-1)
        return lax.dot_general(q_aug, kaug_ref[j, 0:n_keys, :],
                               (((1,), (1,)), ((), ())), preferred_element_type=F32)

    def softmax(s, n_keys):
        n_past = n_keys - tr
        causal = (lax.broadcasted_iota(jnp.int32, (1, tr), 1)
                  <= lax.broadcasted_iota(jnp.int32, (tr, 1), 0))
        own = jnp.where(causal, s[:, n_past:], MASK_VALUE)
        m = jnp.max(own, axis=-1, keepdims=True)
        if n_past:
            past = s[:, :n_past]
            m = jnp.maximum(m, jnp.max(past, axis=-1, keepdims=True))
        p = jnp.exp(own - m)
        if n_past:
            p = jnp.concatenate([jnp.exp(past - m), p], axis=-1)
        return p.astype(BF16), jnp.sum(p, axis=-1, keepdims=True)

    subln_gain = g_ref[...] * (1.0 - lam_init)

    def finish(r0, pv, l):
        o = pv[0] * (1.0 / l[0]) - pv[1] * (lam / l[1])
        ms = jnp.mean(o * o, axis=-1, keepdims=True)
        y = o * lax.rsqrt(ms + SUBLN_EPS) * subln_gain
        gate = gate_ref[pl.ds(r0, tr), :].astype(F32)
        o_ref[pl.ds(r0, tr), :] = (y * gate).astype(o_ref.dtype)

    for c in range(n_q):
        @pl.when(qi == c)
        def _(c=c):
            chains = [(j, r * tr, c * tq + (r + 1) * tr)
                      for r in range(tq // tr) for j in range(2)]
            chains = chains[::-1]
            s_next = logits(*chains[0])
            pv, ls = {}, {}
            for n, (j, r0, n_keys) in enumerate(chains):
                s = s_next
                if n + 1 < len(chains):
                    s_next = logits(*chains[n + 1])
                p, ls[j] = softmax(s, n_keys)
                pv[j] = jnp.dot(p, v_ref[0:n_keys, :], preferred_element_type=F32)
                if len(pv) == 2:
                    finish(r0, pv, ls)
                    pv, ls = {}, {}


def _diff_attention(qkv, gate, lam_vecs, subln_g, *, batch, seq, lam_init, tq=2048, tr=256):
    t_total = qkv.shape[0]
    n_q = seq // tq
    slope_cols, pos_cols = _alibi_tables(seq)
    vec_spec = pl.BlockSpec((1, DIFF_QK), lambda b, h, qi: (0, 0))
    lam_vecs = [v.reshape(1, DIFF_QK).astype(F32) for v in lam_vecs]
    return pl.pallas_call(
        functools.partial(_diff_attn_kernel, tq=tq, tr=tr, n_q=n_q, lam_init=lam_init),
        grid=(batch, DIFF_HEADS, n_q),
        in_specs=[
            vec_spec, vec_spec, vec_spec, vec_spec,
            pl.BlockSpec((1, 1, DIFF_QK), lambda b, h, qi: (h, 0, 0)),
            pl.BlockSpec((seq, DIFF_QK), lambda b, h, qi: (0, 0)),
            pl.BlockSpec((tq, DIFF_V), lambda b, h, qi: (b * n_q + qi, h)),
            pl.BlockSpec((seq, DIFF_V), lambda b, h, qi: (b, DIFF_HEADS + h)),
            pl.BlockSpec((seq, DIFF_V), lambda b, h, qi: (b, 2 * DIFF_HEADS + h)),
            pl.BlockSpec((tq, DIFF_V), lambda b, h, qi: (b * n_q + qi, h)),
            pl.BlockSpec((1, DIFF_V), lambda b, h, qi: (0, 0)),
        ],
        out_specs=pl.BlockSpec((tq, DIFF_V), lambda b, h, qi: (b * n_q + qi, h)),
        out_shape=jax.ShapeDtypeStruct((t_total, D_MIX), BF16),
        scratch_shapes=[pltpu.VMEM((2, seq, 2 * DIFF_QK), BF16)],
        compiler_params=_params("parallel", "parallel", "arbitrary"),
        name="diff_attention",
    )(*lam_vecs, slope_cols, pos_cols, qkv, qkv, qkv, gate,
      subln_g.reshape(1, DIFF_V).astype(F32))


def _mem_attn_kernel(q_ref, k_ref, v_ref, gate_ref, o_ref, *, tr):
    scale = MEM_HEAD_DIM ** -0.5
    chains = [(r * tr, h) for r in range(q_ref.shape[0] // tr) for h in range(MEM_HEADS)]

    def logits(r0, h):
        sl = slice(h * MEM_HEAD_DIM, (h + 1) * MEM_HEAD_DIM)
        return lax.dot_general(q_ref[pl.ds(r0, tr), sl], k_ref[:, sl],
                               (((1,), (1,)), ((), ())), preferred_element_type=F32)

    s_next = logits(*chains[0])
    for n, (r0, h) in enumerate(chains):
        s = s_next * scale
        if n + 1 < len(chains):
            s_next = logits(*chains[n + 1])
        sl = slice(h * MEM_HEAD_DIM, (h + 1) * MEM_HEAD_DIM)
        m = jnp.max(s, axis=-1, keepdims=True)
        p = jnp.exp(s - m)
        l = jnp.sum(p, axis=-1, keepdims=True)
        o = jnp.dot(p.astype(BF16), v_ref[:, sl], preferred_element_type=F32) / l
        rows = pl.ds(r0, tr)
        o_ref[rows, sl] = (o * gate_ref[rows, sl].astype(F32)).astype(o_ref.dtype)


def _mem_attention(proj, q_col, kv, gate, *, batch, seq, mem_len, tq=1024, tr=512):
    t_total = proj.shape[0]
    n_q = seq // tq
    assert q_col % D_MEM == 0
    q_block = q_col // D_MEM
    z_block = D_MIX // D_MEM
    return pl.pallas_call(
        functools.partial(_mem_attn_kernel, tr=tr),
        grid=(batch, n_q),
        in_specs=[
            pl.BlockSpec((tq, D_MEM), lambda b, qi: (b * n_q + qi, q_block)),
            pl.BlockSpec((mem_len, D_MEM), lambda b, qi: (b, 0)),
            pl.BlockSpec((mem_len, D_MEM), lambda b, qi: (b, 1)),
            pl.BlockSpec((tq, D_MEM), lambda b, qi: (b * n_q + qi, z_block)),
        ],
        out_specs=pl.BlockSpec((tq, D_MEM), lambda b, qi: (b * n_q + qi, 0)),
        out_shape=jax.ShapeDtypeStruct((t_total, D_MEM), BF16),
        compiler_params=_params("parallel", "arbitrary"),
        name="mem_attention",
    )(proj, kv, kv, gate)


def _lambda_init(layer_idx):
    return 0.8 - 0.6 * math.exp(-0.3 * layer_idx)


def kernel(x, mem, l0_norm_g, l0_w_in, l0_pool_w, l0_pool_scale, l0_mem_norm_g,
           l0_w_mem_kv, l0_w_out, l1_norm_g, l1_w_in, l1_lambda_q1, l1_lambda_k1,
           l1_lambda_q2, l1_lambda_k2, l1_subln_g, l1_mem_norm_g, l1_w_mem_kv,
           l1_w_out, final_norm_g):
    batch, seq, d = x.shape
    mem_len = mem.shape[1]
    x2 = x.reshape(batch * seq, d)
    mem2 = mem.reshape(batch * mem_len, d)
    dims = dict(batch=batch, seq=seq)

    h = _rmsnorm_ring(x2, l0_norm_g, BF16)
    proj = _matmul(h, l0_w_in, 0, D_MIX + D_MEM)
    gate, w_out = _matmul(h, l0_w_in, D_MIX + D_MEM, D_INNER, gate=True, also_cast=l0_w_out)
    kv = _kv_proj(_rmsnorm(mem2, l0_mem_norm_g, BF16), l0_w_mem_kv)
    y_mix = _pool_mixer(proj, gate, l0_pool_w, l0_pool_scale, **dims)
    y_mem = _mem_attention(proj, D_MIX, kv, gate, mem_len=mem_len, **dims)
    x2 = _out_proj(y_mix, y_mem, w_out, x2)

    h = _rmsnorm_ring(x2, l1_norm_g, BF16)
    proj = _matmul(h, l1_w_in, 0, 3 * D_MIX + D_MEM, scaled_cols=D_MIX,
                   col_scale=DIFF_QK ** -0.5)
    gate, w_out = _matmul(h, l1_w_in, 3 * D_MIX + D_MEM, D_INNER, gate=True,
                          also_cast=l1_w_out)
    kv = _kv_proj(_rmsnorm(mem2, l1_mem_norm_g, BF16), l1_w_mem_kv)
    y_mix = _diff_attention(proj, gate, (l1_lambda_q1, l1_lambda_k1, l1_lambda_q2, l1_lambda_k2),
                            l1_subln_g, lam_init=_lambda_init(1), **dims)
    y_mem = _mem_attention(proj, 3 * D_MIX, kv, gate, mem_len=mem_len, **dims)
    return _out_proj_norm(y_mix, y_mem, w_out, x2, final_norm_g).reshape(batch, seq, d)
```
